```python
import jax, jax.numpy as jnp
from jax import lax
import numpy as np

D_MODEL = 2048
BATCH = 16
SEQ = 2048
DEPTH = 2
DEC_BATCH = 32
DEC_SEQ = 64
PAST_LEN = 4096

CHUNK = 64
N_MIXERS = 2
N_CONV_LAYERS = (DEPTH + 1) // 2
N_ATTN_LAYERS = DEPTH // 2
CONV_WIDTH = 3
N_HEADS = 16
HEAD_DIM = D_MODEL // N_HEADS
PAST_CHUNKS = 8
PAST_WIN = PAST_CHUNKS * CHUNK
BAND = PAST_WIN + CHUNK
MAX_REL = 128
N_REL = 2 * MAX_REL + 1
N_MEM = 256
MEM_HEADS = 4
MEM_HEAD_DIM = 128
MEM_WIDTH = MEM_HEADS * MEM_HEAD_DIM
D_FF = 5632
N_NORMS = 8
EPS = 1e-6
NEG_INF = -1e30

kernel_name = "hybrid_streaming_conv_band_attn_step"


def rmsnorm(x, g):
    xf = x.astype(jnp.float32)
    y = xf * lax.rsqrt(jnp.mean(xf * xf, axis=-1, keepdims=True) + EPS)
    return y.astype(x.dtype) * g


def swiglu(x, w_in, w_out):
    gate, up = jnp.split(x @ w_in, 2, axis=-1)
    return (jax.nn.silu(gate) * up) @ w_out


def short_conv(xn, hist, w_in, w_dw, w_out):
    b, l, d = xn.shape
    bg, cg, xv = jnp.split(xn @ w_in, 3, axis=-1)
    u = cg * xv
    if hist is None:
        hist = jnp.zeros((b, CONV_WIDTH - 1, d), u.dtype)
    u_full = jnp.concatenate([hist.astype(u.dtype), u], axis=1)
    y = lax.conv_general_dilated(u_full, w_dw[:, None, :].astype(u.dtype), window_strides=(1,),
                                 padding='VALID', dimension_numbers=('NWC', 'WIO', 'NWC'),
                                 feature_group_count=d)
    out = (bg * y) @ w_out
    return out, u_full[:, -(CONV_WIDTH - 1):]


def band_attend(q, k, v, q_pos, k_pos, k_valid, rel_bias):
    s = jnp.einsum('bqhd,bkhd->bhqk', q, k).astype(jnp.float32) * (HEAD_DIM ** -0.5)
    rel = jnp.clip(q_pos[:, None] - k_pos[None, :], -MAX_REL, MAX_REL) + MAX_REL
    s = s + rel_bias[:, rel].astype(jnp.float32)[None]
    s = jnp.where(k_valid[None, None, None, :], s, NEG_INF)
    p = jax.nn.softmax(s, axis=-1).astype(v.dtype)
    return jnp.einsum('bhqk,bkhd->bqhd', p, v)


def band_attn_prompt(xn, w_qkv, rel_bias, w_o):
    b, s, _ = xn.shape
    qkv = (xn @ w_qkv).reshape(b, s, 3, N_HEADS, HEAD_DIM)
    q, k, v = qkv[:, :, 0], qkv[:, :, 1], qkv[:, :, 2]
    pad = ((0, 0), (PAST_WIN, 0), (0, 0), (0, 0))
    k_pad, v_pad = jnp.pad(k, pad), jnp.pad(v, pad)
    n_chunks = s // CHUNK

    def one_chunk(n):
        start = n * CHUNK
        qc = lax.dynamic_slice_in_dim(q, start, CHUNK, axis=1)
        kb = lax.dynamic_slice_in_dim(k_pad, start, BAND, axis=1)
        vb = lax.dynamic_slice_in_dim(v_pad, start, BAND, axis=1)
        q_pos = start + jnp.arange(CHUNK)
        k_pos = start - PAST_WIN + jnp.arange(BAND)
        return band_attend(qc, kb, vb, q_pos, k_pos, k_pos >= 0, rel_bias)

    o = lax.map(one_chunk, jnp.arange(n_chunks))
    o = jnp.transpose(o, (1, 0, 2, 3, 4)).reshape(b, s, D_MODEL)
    keep = min(PAST_WIN, s)
    return o @ w_o, k[:, s - keep:], v[:, s - keep:]


def band_attn_sample(xn, ck, cv, w_qkv, rel_bias, w_o):
    b, l, _ = xn.shape
    qkv = (xn @ w_qkv).reshape(b, l, 3, N_HEADS, HEAD_DIM)
    q, k, v = qkv[:, :, 0], qkv[:, :, 1], qkv[:, :, 2]
    pw = ck.shape[1]
    kk = jnp.concatenate([ck.astype(k.dtype), k], axis=1)
    vv = jnp.concatenate([cv.astype(v.dtype), v], axis=1)
    k_pos = PAST_LEN - pw + jnp.arange(pw + l)
    q_pos = PAST_LEN + jnp.arange(l)
    o = band_attend(q, kk, vv, q_pos, k_pos, jnp.ones((pw + l,), bool), rel_bias)
    return o.reshape(b, l, D_MODEL) @ w_o, k, v


def mem_project(mem, g, w_kv):
    b, m, _ = mem.shape
    kv = (rmsnorm(mem, g) @ w_kv).reshape(b, m, 2, MEM_HEADS, MEM_HEAD_DIM)
    return kv[:, :, 0], kv[:, :, 1]


def mem_attend(xn, mk, mv, w_q, w_o):
    b, l, _ = xn.shape
    q = (xn @ w_q).reshape(b, l, MEM_HEADS, MEM_HEAD_DIM)
    s = jnp.einsum('blhd,bmhd->bhlm', q, mk.astype(q.dtype)).astype(jnp.float32) * (MEM_HEAD_DIM ** -0.5)
    p = jax.nn.softmax(s, axis=-1).astype(q.dtype)
    o = jnp.einsum('bhlm,bmhd->blhd', p, mv.astype(q.dtype)).reshape(b, l, MEM_WIDTH)
    return o @ w_o


def trunk(x, conv_states, band_k, band_v, mem_k, mem_v, g_norm,
          w_ffn1_in, w_ffn1_out, w_ffn2_in, w_ffn2_out,
          w_conv_in, w_conv_dw, w_conv_out, w_attn_qkv, rel_bias, w_attn_o,
          w_mem_q, w_mem_o, prompt):
    new_conv, new_bk, new_bv = [], [], []
    for i in range(DEPTH):
        g = g_norm[i]
        h = swiglu(rmsnorm(x, g[0]), w_ffn1_in[i], w_ffn1_out[i])
        x = x + 0.5 * rmsnorm(h, g[1])
        hn = rmsnorm(x, g[2])
        j = i // N_MIXERS
        if i % N_MIXERS == 0:
            hist = None if prompt else conv_states[j]
            h, st = short_conv(hn, hist, w_conv_in[j], w_conv_dw[j], w_conv_out[j])
            new_conv.append(st)
        else:
            if prompt:
                h, nk, nv = band_attn_prompt(hn, w_attn_qkv[j], rel_bias[j], w_attn_o[j])
            else:
                h, nk, nv = band_attn_sample(hn, band_k[j], band_v[j], w_attn_qkv[j], rel_bias[j], w_attn_o[j])
            new_bk.append(nk)
            new_bv.append(nv)
        x = x + rmsnorm(h, g[3])
        h = mem_attend(rmsnorm(x, g[4]), mem_k[i], mem_v[i], w_mem_q[i], w_mem_o[i])
        x = x + rmsnorm(h, g[5])
        h = swiglu(rmsnorm(x, g[6]), w_ffn2_in[i], w_ffn2_out[i])
        x = x + 0.5 * rmsnorm(h, g[7])
    return x, jnp.stack(new_conv), jnp.stack(new_bk), jnp.stack(new_bv)


def setup_inputs(seed: int = 0) -> dict:
    key = jax.random.key(seed)
    ks = jax.random.split(key, 32)
    f32 = jnp.float32

    def nrm(k, shape, fan_in):
        return jax.random.normal(k, shape, f32) * (fan_in ** -0.5)

    pw = min(PAST_WIN, PAST_LEN)
    return {
        "x_prompt": jax.random.normal(ks[0], (BATCH, SEQ, D_MODEL), f32),
        "x_sample": jax.random.normal(ks[1], (DEC_BATCH, DEC_SEQ, D_MODEL), f32),
        "state_conv": jax.random.normal(ks[2], (N_CONV_LAYERS, DEC_BATCH, CONV_WIDTH - 1, D_MODEL), f32),
        "cache_band_k": jax.random.normal(ks[3], (N_ATTN_LAYERS, DEC_BATCH, pw, N_HEADS, HEAD_DIM), f32),
        "cache_band_v": jax.random.normal(ks[4], (N_ATTN_LAYERS, DEC_BATCH, pw, N_HEADS, HEAD_DIM), f32),
        "cache_mem_k": jax.random.normal(ks[5], (DEPTH, DEC_BATCH, N_MEM, MEM_HEADS, MEM_HEAD_DIM), f32),
        "cache_mem_v": jax.random.normal(ks[6], (DEPTH, DEC_BATCH, N_MEM, MEM_HEADS, MEM_HEAD_DIM), f32),
        "mem_prompt": jax.random.normal(ks[7], (BATCH, N_MEM, D_MODEL), f32),
        "g_norm": 1.0 + 0.02 * jax.random.normal(ks[8], (DEPTH, N_NORMS, D_MODEL), f32),
        "g_mem": 1.0 + 0.02 * jax.random.normal(ks[9], (DEPTH, D_MODEL), f32),
        "w_ffn1_in": nrm(ks[10], (DEPTH, D_MODEL, 2 * D_FF), D_MODEL),
        "w_ffn1_out": nrm(ks[11], (DEPTH, D_FF, D_MODEL), D_FF),
        "w_ffn2_in": nrm(ks[12], (DEPTH, D_MODEL, 2 * D_FF), D_MODEL),
        "w_ffn2_out": nrm(ks[13], (DEPTH, D_FF, D_MODEL), D_FF),
        "w_conv_in": nrm(ks[14], (N_CONV_LAYERS, D_MODEL, 3 * D_MODEL), D_MODEL),
        "w_conv_dw": nrm(ks[15], (N_CONV_LAYERS, CONV_WIDTH, D_MODEL), CONV_WIDTH),
        "w_conv_out": nrm(ks[16], (N_CONV_LAYERS, D_MODEL, D_MODEL), D_MODEL),
        "w_attn_qkv": nrm(ks[17], (N_ATTN_LAYERS, D_MODEL, 3 * D_MODEL), D_MODEL),
        "rel_bias": 0.5 * jax.random.normal(ks[18], (N_ATTN_LAYERS, N_HEADS, N_REL), f32),
        "w_attn_o": nrm(ks[19], (N_ATTN_LAYERS, D_MODEL, D_MODEL), D_MODEL),
        "w_mem_q": nrm(ks[20], (DEPTH, D_MODEL, MEM_WIDTH), D_MODEL),
        "w_mem_kv": nrm(ks[21], (DEPTH, D_MODEL, 2 * MEM_WIDTH), D_MODEL),
        "w_mem_o": nrm(ks[22], (DEPTH, MEM_WIDTH, D_MODEL), MEM_WIDTH),
    }


def reference(x_prompt, x_sample, state_conv, cache_band_k, cache_band_v, cache_mem_k, cache_mem_v,
              mem_prompt, g_norm, g_mem, w_ffn1_in, w_ffn1_out, w_ffn2_in, w_ffn2_out,
              w_conv_in, w_conv_dw, w_conv_out, w_attn_qkv, rel_bias, w_attn_o,
              w_mem_q, w_mem_kv, w_mem_o):
    mks, mvs = [], []
    for i in range(DEPTH):
        mk, mv = mem_project(mem_prompt, g_mem[i], w_mem_kv[i])
        mks.append(mk)
        mvs.append(mv)
    mem_k_p = jnp.stack(mks)
    mem_v_p = jnp.stack(mvs)

    y_prompt, conv_p, bk_p, bv_p = trunk(
        x_prompt, None, None, None, mem_k_p, mem_v_p, g_norm,
        w_ffn1_in, w_ffn1_out, w_ffn2_in, w_ffn2_out,
        w_conv_in, w_conv_dw, w_conv_out, w_attn_qkv, rel_bias, w_attn_o,
        w_mem_q, w_mem_o, True)

    y_sample, conv_s, bk_s, bv_s = trunk(
        x_sample, state_conv, cache_band_k, cache_band_v, cache_mem_k, cache_mem_v, g_norm,
        w_ffn1_in, w_ffn1_out, w_ffn2_in, w_ffn2_out,
        w_conv_in, w_conv_dw, w_conv_out, w_attn_qkv, rel_bias, w_attn_o,
        w_mem_q, w_mem_o, False)

    return (y_prompt, y_sample, conv_p, bk_p, bv_p, mem_k_p, mem_v_p, conv_s, bk_s, bv_s)
```

```python
import functools

import jax
import jax.numpy as jnp
from jax import lax
from jax.experimental import pallas as pl
from jax.experimental.pallas import tpu as pltpu

F32 = jnp.float32
BF16 = jnp.bfloat16

D_MODEL = 2048
D_FF = 5632
N_HEADS = 16
HEAD_DIM = 128
CHUNK = 64
PAST_WIN = 512
BAND = PAST_WIN + CHUNK
MAX_REL = 128
N_MEM = 256
MEM_HEADS = 4
MEM_HEAD_DIM = 128
MEM_WIDTH = MEM_HEADS * MEM_HEAD_DIM
EPS = 1e-6
NEG_INF = -1e30

V7X_VMEM_LIMIT_BYTES = 58 * 1024 * 1024
BF16_SUBLANES = 16
FF_TILE = 512
CONV_TILE = 512
QUERY_GROUP = 4 * CHUNK


def _params(sem):
    return pltpu.CompilerParams(dimension_semantics=sem, vmem_limit_bytes=V7X_VMEM_LIMIT_BYTES)


def _const_spec(shape):
    return pl.BlockSpec(shape, lambda *_: (0,) * len(shape), pipeline_mode=pl.Buffered(1))


def _rms(x, g):
    return x * lax.rsqrt(jnp.mean(x * x, axis=-1, keepdims=True) + EPS) * g


def _rms_cast_kernel(x_ref, g_ref, o_ref):
    o_ref[...] = _rms(x_ref[...], g_ref[...]).astype(BF16)


def _rms_cast(x, g, tm):
    t, d = x.shape
    return pl.pallas_call(
        _rms_cast_kernel,
        out_shape=jax.ShapeDtypeStruct((t, d), BF16),
        grid=(t // tm,),
        in_specs=[pl.BlockSpec((tm, d), lambda i: (i, 0)), pl.BlockSpec((1, d), lambda i: (0, 0))],
        out_specs=pl.BlockSpec((tm, d), lambda i: (i, 0)),
        compiler_params=_params(("arbitrary",)),
        name="rms_cast",
    )(x, g)


def _res_norm_epilogue(acc_ref, x_ref, gp_ref, gn_ref, xo_ref, xno_ref, scale):
    gp = gp_ref[...]
    gn = gn_ref[...] if xno_ref is not None else None
    rows = BF16_SUBLANES

    def body(r, carry):
        sl = pl.ds(pl.multiple_of(r * rows, rows), rows)
        y = _rms(acc_ref[sl, :], gp)
        if scale != 1.0:
            y = scale * y
        xnew = x_ref[sl, :] + y
        xo_ref[sl, :] = xnew
        if xno_ref is not None:
            xno_ref[sl, :] = _rms(xnew, gn).astype(BF16)
        return carry

    lax.fori_loop(0, acc_ref.shape[0] // rows, body, 0)


def _mm_res_norm_kernel(a_ref, w_ref, x_ref, gp_ref, gn_ref, xo_ref, *rest, scale, emit_xn):
    xno_ref, acc_ref = rest if emit_xn else (None, rest[0])
    acc_ref[...] = jnp.dot(a_ref[...], w_ref[...], preferred_element_type=F32)
    _res_norm_epilogue(acc_ref, x_ref, gp_ref, gn_ref, xo_ref, xno_ref, scale)


def _mm_res_norm(a, w, x, g_post, g_next, scale, tm, emit_xn=True):
    t, k = a.shape
    d = w.shape[1]
    row = lambda i: (i, 0)
    out_shape = [jax.ShapeDtypeStruct((t, d), F32)]
    out_specs = [pl.BlockSpec((tm, d), row)]
    if emit_xn:
        out_shape.append(jax.ShapeDtypeStruct((t, d), BF16))
        out_specs.append(pl.BlockSpec((tm, d), row))
    outs = pl.pallas_call(
        functools.partial(_mm_res_norm_kernel, scale=scale, emit_xn=emit_xn),
        out_shape=out_shape,
        grid=(t // tm,),
        in_specs=[
            pl.BlockSpec((tm, k), row),
            _const_spec((k, d)),
            pl.BlockSpec((tm, d), row),
            _const_spec((1, d)),
            _const_spec((1, d)),
        ],
        out_specs=out_specs,
        scratch_shapes=[pltpu.VMEM((tm, d), F32)],
        compiler_params=_params(("arbitrary",)),
        name="mm_res_norm",
    )(a, w, x, g_post, g_next)
    return (outs[0], outs[1]) if emit_xn else (outs[0], None)


def _ffn_in_kernel(xn_ref, w_ref, o_ref):
    h = jnp.dot(xn_ref[...], w_ref[...], preferred_element_type=F32)
    o_ref[...] = (jax.nn.silu(h[:, :FF_TILE]) * h[:, FF_TILE:]).astype(BF16)


def _ffn_in(xn, w_tiled, tm):
    t, d = xn.shape
    return pl.pallas_call(
        _ffn_in_kernel,
        out_shape=jax.ShapeDtypeStruct((t, D_FF), BF16),
        grid=(t // tm, D_FF // FF_TILE),
        in_specs=[
            pl.BlockSpec((tm, d), lambda i, j: (i, 0)),
            pl.BlockSpec((d, 2 * FF_TILE), lambda i, j: (0, j)),
        ],
        out_specs=pl.BlockSpec((tm, FF_TILE), lambda i, j: (i, j)),
        compiler_params=_params(("arbitrary", "arbitrary")),
        name="ffn_in",
    )(xn, w_tiled)


def _mm_kernel(a_ref, w_ref, o_ref):
    o_ref[...] = jnp.dot(a_ref[...], w_ref[...], preferred_element_type=F32)


def _mm(a, w, tm, tn):
    t, k = a.shape
    n = w.shape[1]
    return pl.pallas_call(
        _mm_kernel,
        out_shape=jax.ShapeDtypeStruct((t, n), F32),
        grid=(t // tm, n // tn),
        in_specs=[pl.BlockSpec((tm, k), lambda i, j: (i, 0)), pl.BlockSpec((k, tn), lambda i, j: (0, j))],
        out_specs=pl.BlockSpec((tm, tn), lambda i, j: (i, j)),
        compiler_params=_params(("arbitrary", "arbitrary")),
        name="mm",
    )(a, w)


def _gated_conv(h, hist0, hist1, wdw, seq):
    tm = h.shape[0]
    nseq = tm // seq
    bg = h[:, :CONV_TILE]
    u = h[:, CONV_TILE:2 * CONV_TILE] * h[:, 2 * CONV_TILE:]
    u3 = u.reshape(nseq, seq, CONV_TILE)
    row = lax.broadcasted_iota(jnp.int32, u3.shape, 1)
    prev1 = pltpu.roll(u, 1, axis=0).reshape(u3.shape)
    prev2 = pltpu.roll(u, 2, axis=0).reshape(u3.shape)
    prev1 = jnp.where(row == 0, hist1, prev1)
    prev2 = jnp.where(row == 0, hist0, jnp.where(row == 1, hist1, prev2))
    y = wdw[0:1, :] * prev2 + wdw[1:2, :] * prev1 + wdw[2:3, :] * u3
    z = (bg.reshape(u3.shape) * y).astype(BF16).reshape(tm, CONV_TILE)
    return z, u3


def _conv_in_prompt_kernel(xn_ref, w_ref, wdw_ref, z_ref, st_ref, carry_ref, *, tiles_per_seq):
    tm = xn_ref.shape[0]

    @pl.when(pl.program_id(1) % tiles_per_seq == 0)
    def _():
        carry_ref[...] = jnp.zeros_like(carry_ref)

    h = jnp.dot(xn_ref[...], w_ref[...], preferred_element_type=F32)
    hist0 = carry_ref[0:1, :].reshape(1, 1, CONV_TILE)
    hist1 = carry_ref[1:2, :].reshape(1, 1, CONV_TILE)
    z, u3 = _gated_conv(h, hist0, hist1, wdw_ref[...], tm)
    z_ref[...] = z
    last = u3[0, tm - 2:, :]
    carry_ref[0:2, :] = last
    st_ref[0] = last


def _conv_in_prompt(xn, w_tiled, wdw, batch, seq, tm):
    t, d = xn.shape
    tiles_per_seq = seq // tm
    return pl.pallas_call(
        functools.partial(_conv_in_prompt_kernel, tiles_per_seq=tiles_per_seq),
        out_shape=[jax.ShapeDtypeStruct((t, d), BF16), jax.ShapeDtypeStruct((batch, 2, d), F32)],
        grid=(d // CONV_TILE, t // tm),
        in_specs=[
            pl.BlockSpec((tm, d), lambda j, i: (i, 0)),
            pl.BlockSpec((d, 3 * CONV_TILE), lambda j, i: (0, j)),
            pl.BlockSpec((3, CONV_TILE), lambda j, i: (0, j)),
        ],
        out_specs=[
            pl.BlockSpec((tm, CONV_TILE), lambda j, i: (i, j)),
            pl.BlockSpec((1, 2, CONV_TILE), lambda j, i: (i // tiles_per_seq, 0, j)),
        ],
        scratch_shapes=[pltpu.VMEM((8, CONV_TILE), F32)],
        compiler_params=_params(("arbitrary", "arbitrary")),
        name="conv_in_prompt",
    )(xn, w_tiled, wdw)


def _conv_in_sample_kernel(xn_ref, w_ref, wdw_ref, hist_ref, z_ref, st_ref, *, seq):
    h = jnp.dot(xn_ref[...], w_ref[...], preferred_element_type=F32)
    hist = hist_ref[...]
    z, u3 = _gated_conv(h, hist[:, 0:1, :], hist[:, 1:2, :], wdw_ref[...], seq)
    z_ref[...] = z
    st_ref[...] = u3[:, seq - 2:, :]


def _conv_in_sample(xn, w_tiled, wdw, hist, seq, tm):
    t, d = xn.shape
    nseq = tm // seq
    return pl.pallas_call(
        functools.partial(_conv_in_sample_kernel, seq=seq),
        out_shape=[jax.ShapeDtypeStruct((t, d), BF16), jax.ShapeDtypeStruct(hist.shape, F32)],
        grid=(d // CONV_TILE, t // tm),
        in_specs=[
            pl.BlockSpec((tm, d), lambda j, i: (i, 0)),
            pl.BlockSpec((d, 3 * CONV_TILE), lambda j, i: (0, j)),
            pl.BlockSpec((3, CONV_TILE), lambda j, i: (0, j)),
            pl.BlockSpec((nseq, 2, CONV_TILE), lambda j, i: (i, 0, j)),
        ],
        out_specs=[
            pl.BlockSpec((tm, CONV_TILE), lambda j, i: (i, j)),
            pl.BlockSpec((nseq, 2, CONV_TILE), lambda j, i: (i, 0, j)),
        ],
        compiler_params=_params(("arbitrary", "arbitrary")),
        name="conv_in_sample",
    )(xn, w_tiled, wdw, hist)


def _qk(q, k):
    return lax.dot_general(q, k, (((1,), (1,)), ((), ())), preferred_element_type=F32)


def _band_attn_prompt_kernel(q_ref, k_ref, v_ref, bias_ref, o_ref):
    seq = q_ref.shape[0]
    sm_scale = HEAD_DIM ** -0.5
    for m in range(seq // QUERY_GROUP):
        q0 = m * QUERY_GROUP
        lo = max(0, q0 - PAST_WIN)
        hi = q0 + QUERY_GROUP
        col0 = lo - (q0 - PAST_WIN)
        q = q_ref[q0:hi, :].astype(BF16)
        k = k_ref[lo:hi, :].astype(BF16)
        v = v_ref[lo:hi, :].astype(BF16)
        s = _qk(q, k) * sm_scale + bias_ref[0, :, col0:col0 + hi - lo]
        p = jnp.exp(s - jnp.max(s, axis=-1, keepdims=True))
        l = jnp.sum(p, axis=-1, keepdims=True)
        o = jnp.dot(p.astype(BF16), v, preferred_element_type=F32) / l
        o_ref[q0:hi, :] = o.astype(BF16)


def _band_attn_prompt(qkv, bias, batch, seq):
    return pl.pallas_call(
        _band_attn_prompt_kernel,
        out_shape=jax.ShapeDtypeStruct((batch * seq, D_MODEL), BF16),
        grid=(N_HEADS, batch),
        in_specs=[
            pl.BlockSpec((seq, HEAD_DIM), lambda h, b: (b, h)),
            pl.BlockSpec((seq, HEAD_DIM), lambda h, b: (b, N_HEADS + h)),
            pl.BlockSpec((seq, HEAD_DIM), lambda h, b: (b, 2 * N_HEADS + h)),
            pl.BlockSpec((1,) + bias.shape[1:], lambda h, b: (h, 0, 0)),
        ],
        out_specs=pl.BlockSpec((seq, HEAD_DIM), lambda h, b: (b, h)),
        compiler_params=_params(("arbitrary", "arbitrary")),
        name="band_attn_prompt",
    )(qkv, qkv, qkv, bias)


def _band_attn_sample_kernel(qkv_ref, ck_ref, cv_ref, bias_c_ref, bias_n_ref, o_ref):
    sm_scale = HEAD_DIM ** -0.5
    for h in range(N_HEADS):
        c = slice(h * HEAD_DIM, (h + 1) * HEAD_DIM)
        q = qkv_ref[:, c].astype(BF16)
        kn = qkv_ref[:, D_MODEL + h * HEAD_DIM:D_MODEL + (h + 1) * HEAD_DIM].astype(BF16)
        vn = qkv_ref[:, 2 * D_MODEL + h * HEAD_DIM:2 * D_MODEL + (h + 1) * HEAD_DIM].astype(BF16)
        kc = ck_ref[:, c].astype(BF16)
        vc = cv_ref[:, c].astype(BF16)
        s_c = _qk(q, kc) * sm_scale + bias_c_ref[h]
        s_n = _qk(q, kn) * sm_scale + bias_n_ref[h]
        mx = jnp.maximum(jnp.max(s_c, axis=-1, keepdims=True), jnp.max(s_n, axis=-1, keepdims=True))
        p_c = jnp.exp(s_c - mx)
        p_n = jnp.exp(s_n - mx)
        l = jnp.sum(p_c, axis=-1, keepdims=True) + jnp.sum(p_n, axis=-1, keepdims=True)
        o = (jnp.dot(p_c.astype(BF16), vc, preferred_element_type=F32)
             + jnp.dot(p_n.astype(BF16), vn, preferred_element_type=F32)) / l
        o_ref[:, c] = o.astype(BF16)


def _band_attn_sample(qkv, ck, cv, bias_c, bias_n, batch, seq):
    pw = ck.shape[0] // batch
    return pl.pallas_call(
        _band_attn_sample_kernel,
        out_shape=jax.ShapeDtypeStruct((batch * seq, D_MODEL), BF16),
        grid=(batch,),
        in_specs=[
            pl.BlockSpec((seq, 3 * D_MODEL), lambda b: (b, 0)),
            pl.BlockSpec((pw, D_MODEL), lambda b: (b, 0)),
            pl.BlockSpec((pw, D_MODEL), lambda b: (b, 0)),
            _const_spec(bias_c.shape),
            _const_spec(bias_n.shape),
        ],
        out_specs=pl.BlockSpec((seq, D_MODEL), lambda b: (b, 0)),
        compiler_params=_params(("arbitrary",)),
        name="band_attn_sample",
    )(qkv, ck, cv, bias_c, bias_n)


def _band_bias_table(rel_bias):
    qi = jnp.arange(QUERY_GROUP)[:, None]
    kj = jnp.arange(PAST_WIN + QUERY_GROUP)[None, :]
    band_lo = (qi // CHUNK) * CHUNK
    in_band = (kj >= band_lo) & (kj < band_lo + BAND)
    rel = jnp.clip(qi + PAST_WIN - kj, -MAX_REL, MAX_REL) + MAX_REL
    return jnp.where(in_band[None], rel_bias[:, rel], NEG_INF).astype(F32)


def _mem_attn_kernel(xn_ref, wq_ref, mk_ref, mv_ref, wo_ref, x_ref, gp_ref, gn_ref,
                     xo_ref, xno_ref, acc_ref):
    sm_scale = MEM_HEAD_DIM ** -0.5
    q = jnp.dot(xn_ref[...], wq_ref[...], preferred_element_type=F32)
    heads = []
    for h in range(MEM_HEADS):
        c = slice(h * MEM_HEAD_DIM, (h + 1) * MEM_HEAD_DIM)
        s = _qk(q[:, c].astype(BF16), mk_ref[:, c].astype(BF16)) * sm_scale
        p = jnp.exp(s - jnp.max(s, axis=-1, keepdims=True))
        l = jnp.sum(p, axis=-1, keepdims=True)
        o = jnp.dot(p.astype(BF16), mv_ref[:, c].astype(BF16), preferred_element_type=F32) / l
        heads.append(o.astype(BF16))
    o = jnp.concatenate(heads, axis=1)
    acc_ref[...] = jnp.dot(o, wo_ref[...], preferred_element_type=F32)
    _res_norm_epilogue(acc_ref, x_ref, gp_ref, gn_ref, xo_ref, xno_ref, 1.0)


def _mem_attn(xn, x, mk, mv, wq, wo, g_post, g_next, batch, tm):
    t, d = x.shape
    tiles_per_seq = t // batch // tm
    row = lambda b, i: (b * tiles_per_seq + i, 0)
    mem = lambda b, i: (b, 0)
    return pl.pallas_call(
        _mem_attn_kernel,
        out_shape=[jax.ShapeDtypeStruct((t, d), F32), jax.ShapeDtypeStruct((t, d), BF16)],
        grid=(batch, tiles_per_seq),
        in_specs=[
            pl.BlockSpec((tm, d), row),
            _const_spec((d, MEM_WIDTH)),
            pl.BlockSpec((N_MEM, MEM_WIDTH), mem),
            pl.BlockSpec((N_MEM, MEM_WIDTH), mem),
            _const_spec((MEM_WIDTH, d)),
            pl.BlockSpec((tm, d), row),
            _const_spec((1, d)),
            _const_spec((1, d)),
        ],
        out_specs=[pl.BlockSpec((tm, d), row), pl.BlockSpec((tm, d), row)],
        scratch_shapes=[pltpu.VMEM((tm, d), F32)],
        compiler_params=_params(("arbitrary", "arbitrary")),
        name="mem_attn",
    )(xn, wq, mk, mv, wo, x, g_post, g_next)


def _mem_project_kernel(mem_ref, g_ref, w_ref, k_ref, v_ref):
    kv = jnp.dot(_rms(mem_ref[...], g_ref[...]).astype(BF16), w_ref[...], preferred_element_type=F32)
    k_ref[...] = kv[:, :MEM_WIDTH]
    v_ref[...] = kv[:, MEM_WIDTH:]


def _mem_project(mem, g_mem, w_kv, tm):
    t, d = mem.shape
    depth = w_kv.shape[0]
    out = jax.ShapeDtypeStruct((depth, t, MEM_WIDTH), F32)
    out_spec = pl.BlockSpec((None, tm, MEM_WIDTH), lambda l, i: (l, i, 0))
    return pl.pallas_call(
        _mem_project_kernel,
        out_shape=[out, out],
        grid=(depth, t // tm),
        in_specs=[
            pl.BlockSpec((tm, d), lambda l, i: (i, 0)),
            pl.BlockSpec((None, 1, d), lambda l, i: (l, 0, 0)),
            pl.BlockSpec((None, d, 2 * MEM_WIDTH), lambda l, i: (l, 0, 0)),
        ],
        out_specs=[out_spec, out_spec],
        compiler_params=_params(("arbitrary", "arbitrary")),
        name="mem_project",
    )(mem, g_mem, w_kv)


def _tile_columns(w, parts, tile):
    k, n = w.shape[0], w.shape[1] // parts
    w = w.astype(BF16).reshape(k, parts, n // tile, tile)
    return jnp.swapaxes(w, 1, 2).reshape(k, parts * n)


def _trunk(x, mem_k, mem_v, weights, *, batch, seq, conv_hist, band_cache, bias_tables):
    prompt = conv_hist is None
    (g_norm, w_ffn1_in, w_ffn1_out, w_ffn2_in, w_ffn2_out, w_conv_in, w_conv_dw, w_conv_out,
     w_attn_qkv, w_attn_o, w_mem_q, w_mem_o) = weights
    depth = g_norm.shape[0]
    tm_in = 1024
    tm_ffn_out = 256
    tm_mix_out = 512
    tm_mem = 512 if prompt else seq
    g = lambda i, n: g_norm[i % depth, n][None, :]

    xn = _rms_cast(x, g(0, 0), tm_in)
    conv_state = band_k = band_v = None
    for i in range(depth):
        a = _ffn_in(xn, w_ffn1_in[i], tm_in)
        x, xn = _mm_res_norm(a, w_ffn1_out[i], x, g(i, 1), g(i, 2), 0.5, tm_ffn_out)
        if i % 2 == 0:
            if prompt:
                z, conv_state = _conv_in_prompt(xn, w_conv_in[i // 2], w_conv_dw[i // 2], batch, seq, tm_in)
            else:
                z, conv_state = _conv_in_sample(xn, w_conv_in[i // 2], w_conv_dw[i // 2], conv_hist, seq, tm_in)
            x, xn = _mm_res_norm(z, w_conv_out[i // 2], x, g(i, 3), g(i, 4), 1.0, tm_mix_out)
        else:
            qkv = _mm(xn, w_attn_qkv[i // 2], tm_in, 1024)
            if prompt:
                o = _band_attn_prompt(qkv, bias_tables[0], batch, seq)
                keep = min(PAST_WIN, seq)
                qkv3 = qkv.reshape(batch, seq, 3 * D_MODEL)
                band_k = qkv3[:, seq - keep:, D_MODEL:2 * D_MODEL].reshape(batch, keep, N_HEADS, HEAD_DIM)
                band_v = qkv3[:, seq - keep:, 2 * D_MODEL:].reshape(batch, keep, N_HEADS, HEAD_DIM)
            else:
                ck, cv = band_cache
                o = _band_attn_sample(qkv, ck, cv, bias_tables[1], bias_tables[2], batch, seq)
                band_k = qkv[:, D_MODEL:2 * D_MODEL].reshape(batch, seq, N_HEADS, HEAD_DIM)
                band_v = qkv[:, 2 * D_MODEL:].reshape(batch, seq, N_HEADS, HEAD_DIM)
            x, xn = _mm_res_norm(o, w_attn_o[i // 2], x, g(i, 3), g(i, 4), 1.0, tm_mix_out)
        x, xn = _mem_attn(xn, x, mem_k[i], mem_v[i], w_mem_q[i], w_mem_o[i], g(i, 5), g(i, 6), batch, tm_mem)
        a = _ffn_in(xn, w_ffn2_in[i], tm_in)
        last = i == depth - 1
        x, xn = _mm_res_norm(a, w_ffn2_out[i], x, g(i, 7), g(i + 1, 0), 0.5, tm_ffn_out, emit_xn=not last)
    return x, conv_state, band_k, band_v


def kernel(x_prompt, x_sample, state_conv, cache_band_k, cache_band_v, cache_mem_k, cache_mem_v, mem_prompt, g_norm, g_mem, w_ffn1_in, w_ffn1_out, w_ffn2_in, w_ffn2_out, w_conv_in, w_conv_dw, w_conv_out, w_attn_qkv, rel_bias, w_attn_o, w_mem_q, w_mem_kv, w_mem_o):
    batch, seq, d = x_prompt.shape
    dec_batch, dec_seq, _ = x_sample.shape
    depth = g_norm.shape[0]
    assert depth == 2 and state_conv.shape[0] == 1 and cache_band_k.shape[0] == 1
    assert cache_band_k.shape[2] == PAST_WIN and dec_seq == CHUNK and seq % QUERY_GROUP == 0

    tile_ffn = jax.vmap(lambda w: _tile_columns(w, 2, FF_TILE))
    tile_conv = jax.vmap(lambda w: _tile_columns(w, 3, CONV_TILE))
    weights = (
        g_norm,
        tile_ffn(w_ffn1_in), w_ffn1_out.astype(BF16),
        tile_ffn(w_ffn2_in), w_ffn2_out.astype(BF16),
        tile_conv(w_conv_in), w_conv_dw, w_conv_out.astype(BF16),
        w_attn_qkv.astype(BF16), w_attn_o.astype(BF16),
        w_mem_q.astype(BF16), w_mem_o.astype(BF16),
    )
    table = _band_bias_table(rel_bias[0])
    bias_tables = (table, table[:, :CHUNK, :PAST_WIN], table[:, :CHUNK, PAST_WIN:BAND])

    mem_k_p, mem_v_p = _mem_project(mem_prompt.reshape(batch * N_MEM, d), g_mem[:, None, :],
                                    w_mem_kv.astype(BF16), 512)

    y_p, conv_p, bk_p, bv_p = _trunk(
        x_prompt.reshape(batch * seq, d), mem_k_p, mem_v_p, weights,
        batch=batch, seq=seq, conv_hist=None, band_cache=None, bias_tables=bias_tables)

    y_s, conv_s, bk_s, bv_s = _trunk(
        x_sample.reshape(dec_batch * dec_seq, d),
        cache_mem_k.reshape(depth, dec_batch * N_MEM, MEM_WIDTH),
        cache_mem_v.reshape(depth, dec_batch * N_MEM, MEM_WIDTH), weights,
        batch=dec_batch, seq=dec_seq, conv_hist=state_conv[0],
        band_cache=(cache_band_k[0].reshape(dec_batch * PAST_WIN, d),
                    cache_band_v[0].reshape(dec_batch * PAST_WIN, d)),
        bias_tables=bias_tables)

    mem_shape = (depth, batch, N_MEM, MEM_HEADS, MEM_HEAD_DIM)
    return (y_p.reshape(batch, seq, d), y_s.reshape(dec_batch, dec_seq, d),
            conv_p[None], bk_p[None], bv_p[None],
            mem_k_p.reshape(mem_shape), mem_v_p.reshape(mem_shape),
            conv_s[None], bk_s[None], bv_s[None])
```

```python
import functools

import jax
import jax.numpy as jnp
from jax import lax
from jax.experimental import pallas as pl
from jax.experimental.pallas import tpu as pltpu

F32 = jnp.float32
BF16 = jnp.bfloat16

D_MODEL = 2048
D_FF = 5632
N_HEADS = 16
HEAD_DIM = 128
CHUNK = 64
PAST_WIN = 512
BAND = PAST_WIN + CHUNK
MAX_REL = 128
N_MEM = 256
MEM_HEADS = 4
MEM_HEAD_DIM = 128
MEM_WIDTH = MEM_HEADS * MEM_HEAD_DIM
EPS = 1e-6
NEG_INF = -1e30

V7X_VMEM_LIMIT_BYTES = 58 * 1024 * 1024
BF16_SUBLANES = 16
FF_TILE = 512
CONV_TILE = 512
QKV_TILE = 1024
QUERY_GROUP = 4 * CHUNK
TM_IN = 1024
TM_FFN_OUT = 256
TM_MIX_OUT = 512
TM_MEM = 512


def _params(sem):
    return pltpu.CompilerParams(dimension_semantics=sem, vmem_limit_bytes=V7X_VMEM_LIMIT_BYTES)


def _layer_spec(shape, layer):
    zeros = (0,) * len(shape)
    return pl.BlockSpec((None,) + tuple(shape), lambda *_: (layer,) + zeros, pipeline_mode=pl.Buffered(1))


def _gain_spec(layer, n):
    return pl.BlockSpec((None, None, 1, D_MODEL), lambda *_: (layer, n, 0, 0), pipeline_mode=pl.Buffered(1))


def _rms(x, g):
    return x * lax.rsqrt(jnp.mean(x * x, axis=-1, keepdims=True) + EPS) * g


def _rms_cast_kernel(x_ref, g_ref, o_ref):
    o_ref[...] = _rms(x_ref[...], g_ref[...]).astype(BF16)


def _rms_cast(x, g4):
    t, d = x.shape
    return pl.pallas_call(
        _rms_cast_kernel,
        out_shape=jax.ShapeDtypeStruct((t, d), BF16),
        grid=(t // TM_IN,),
        in_specs=[pl.BlockSpec((TM_IN, d), lambda i: (i, 0)), _gain_spec(0, 0)],
        out_specs=pl.BlockSpec((TM_IN, d), lambda i: (i, 0)),
        compiler_params=_params(("arbitrary",)),
        name="rms_cast",
    )(x, g4)


def _res_norm_epilogue(acc_ref, x_ref, gp, gn, xo_ref, xno_ref):
    rows = BF16_SUBLANES
    for r in range(acc_ref.shape[0] // rows):
        sl = slice(r * rows, (r + 1) * rows)
        xnew = x_ref[sl, :] + _rms(acc_ref[sl, :], gp)
        xo_ref[sl, :] = xnew
        if xno_ref is not None:
            xno_ref[sl, :] = _rms(xnew, gn).astype(BF16)


def _skewed_step(compute_tile, x_ref, gp_ref, gn_ref, xo_ref, xno_ref, acc0_ref, acc1_ref, scale):
    i = pl.program_id(0)

    @pl.when(i == 0)
    def _():
        acc1_ref[...] = jnp.zeros_like(acc1_ref)

    def step(prev_ref, cur_ref):
        gp = gp_ref[...] * scale
        gn = gn_ref[...]
        _res_norm_epilogue(prev_ref, x_ref, gp, gn, xo_ref, xno_ref)
        cur_ref[...] = compute_tile()

    @pl.when(i % 2 == 0)
    def _():
        step(acc1_ref, acc0_ref)

    @pl.when(i % 2 == 1)
    def _():
        step(acc0_ref, acc1_ref)


def _load_rows(a_ref):
    if len(a_ref.shape) == 2:
        return a_ref[...]
    return jnp.concatenate([a_ref[h] for h in range(a_ref.shape[0])], axis=1)


def _mm_res_norm_kernel(a_ref, w_ref, x_ref, gp_ref, gn_ref, xo_ref, *rest, scale, emit_xn):
    xno_ref, acc0_ref, acc1_ref = rest if emit_xn else (None,) + rest
    compute = lambda: jnp.dot(_load_rows(a_ref), w_ref[...], preferred_element_type=F32)
    _skewed_step(compute, x_ref, gp_ref, gn_ref, xo_ref, xno_ref, acc0_ref, acc1_ref, scale)


def _mm_res_norm(a, w, w_layer, x, g4, post_gain, next_gain, scale, tm, emit_xn=True):
    t, d = x.shape
    k = w.shape[1]
    n = t // tm
    cur = lambda i: jnp.minimum(i, n - 1)
    prev = lambda i: (jnp.maximum(i - 1, 0), 0)
    if a.ndim == 2:
        a_spec = pl.BlockSpec((tm, k), lambda i: (cur(i), 0))
    else:
        a_spec = pl.BlockSpec((a.shape[0], tm, a.shape[2]), lambda i: (0, cur(i), 0))
    out_shape = [jax.ShapeDtypeStruct((t, d), F32)]
    out_specs = [pl.BlockSpec((tm, d), prev)]
    if emit_xn:
        out_shape.append(jax.ShapeDtypeStruct((t, d), BF16))
        out_specs.append(pl.BlockSpec((tm, d), prev))
    outs = pl.pallas_call(
        functools.partial(_mm_res_norm_kernel, scale=scale, emit_xn=emit_xn),
        out_shape=out_shape,
        grid=(n + 1,),
        in_specs=[
            a_spec,
            _layer_spec((k, d), w_layer),
            pl.BlockSpec((tm, d), prev),
            _gain_spec(*post_gain),
            _gain_spec(*next_gain),
        ],
        out_specs=out_specs,
        scratch_shapes=[pltpu.VMEM((tm, d), F32), pltpu.VMEM((tm, d), F32)],
        compiler_params=_params(("arbitrary",)),
        name="mm_res_norm",
    )(a, w, x, g4, g4)
    return (outs[0], outs[1]) if emit_xn else (outs[0], None)


def _ffn_in_kernel(xn_ref, wg_ref, wu_ref, o_ref, w_ref):
    @pl.when(pl.program_id(1) == 0)
    def _():
        w_ref[:, :FF_TILE] = wg_ref[...].astype(BF16)
        w_ref[:, FF_TILE:] = wu_ref[...].astype(BF16)

    h = jnp.dot(xn_ref[...], w_ref[...], preferred_element_type=F32)
    o_ref[...] = (jax.nn.silu(h[:, :FF_TILE]) * h[:, FF_TILE:]).astype(BF16)


def _ffn_in(xn, w_in, layer):
    t, d = xn.shape
    nj = D_FF // FF_TILE
    return pl.pallas_call(
        _ffn_in_kernel,
        out_shape=jax.ShapeDtypeStruct((t, D_FF), BF16),
        grid=(nj, t // TM_IN),
        in_specs=[
            pl.BlockSpec((TM_IN, d), lambda j, i: (i, 0)),
            pl.BlockSpec((None, d, FF_TILE), lambda j, i: (layer, 0, j)),
            pl.BlockSpec((None, d, FF_TILE), lambda j, i: (layer, 0, nj + j)),
        ],
        out_specs=pl.BlockSpec((TM_IN, FF_TILE), lambda j, i: (i, j)),
        scratch_shapes=[pltpu.VMEM((d, 2 * FF_TILE), BF16)],
        compiler_params=_params(("arbitrary", "arbitrary")),
        name="ffn_in",
    )(xn, w_in, w_in)


def _qkv_kernel(xn_ref, wf_ref, hm_ref, kv_ref, w_ref, *, keep_every):
    j, i = pl.program_id(0), pl.program_id(1)

    @pl.when(i == 0)
    def _():
        w_ref[...] = wf_ref[...].astype(BF16)

    acc = jnp.dot(xn_ref[...], w_ref[...], preferred_element_type=F32)
    for h in range(QKV_TILE // HEAD_DIM):
        hm_ref[h] = acc[:, h * HEAD_DIM:(h + 1) * HEAD_DIM].astype(BF16)

    @pl.when((j >= D_MODEL // QKV_TILE) & (i % keep_every == keep_every - 1))
    def _():
        kv_ref[0] = acc[acc.shape[0] - kv_ref.shape[1]:, :]


def _qkv(xn, w_qkv, layer, keep_rows, keep_every):
    t, d = xn.shape
    nq = D_MODEL // QKV_TILE
    heads_per_tile = QKV_TILE // HEAD_DIM
    n_keep = t // (TM_IN * keep_every)

    def kv_index(j, i):
        return (jnp.where(j < nq, 0, i // keep_every), 0, jnp.maximum(j - nq, 0))

    return pl.pallas_call(
        functools.partial(_qkv_kernel, keep_every=keep_every),
        out_shape=[jax.ShapeDtypeStruct((3 * N_HEADS, t, HEAD_DIM), BF16),
                   jax.ShapeDtypeStruct((n_keep, keep_rows, 2 * D_MODEL), F32)],
        grid=(3 * nq, t // TM_IN),
        in_specs=[
            pl.BlockSpec((TM_IN, d), lambda j, i: (i, 0)),
            pl.BlockSpec((None, d, QKV_TILE), lambda j, i: (layer, 0, j), pipeline_mode=pl.Buffered(1)),
        ],
        out_specs=[
            pl.BlockSpec((heads_per_tile, TM_IN, HEAD_DIM), lambda j, i: (j, i, 0)),
            pl.BlockSpec((1, keep_rows, QKV_TILE), kv_index),
        ],
        scratch_shapes=[pltpu.VMEM((d, QKV_TILE), BF16)],
        compiler_params=_params(("arbitrary", "arbitrary")),
        name="qkv",
    )(xn, w_qkv)


def _cast_conv_weights(wb_ref, wc_ref, wx_ref, w_ref):
    @pl.when(pl.program_id(1) == 0)
    def _():
        w_ref[:, :CONV_TILE] = wb_ref[...].astype(BF16)
        w_ref[:, CONV_TILE:2 * CONV_TILE] = wc_ref[...].astype(BF16)
        w_ref[:, 2 * CONV_TILE:] = wx_ref[...].astype(BF16)


def _gated_conv(h, hist0, hist1, wdw, seq):
    tm = h.shape[0]
    nseq = tm // seq
    bg = h[:, :CONV_TILE]
    u = h[:, CONV_TILE:2 * CONV_TILE] * h[:, 2 * CONV_TILE:]
    u3 = u.reshape(nseq, seq, CONV_TILE)
    row = lax.broadcasted_iota(jnp.int32, u3.shape, 1)
    prev1 = pltpu.roll(u, 1, axis=0).reshape(u3.shape)
    prev2 = pltpu.roll(u, 2, axis=0).reshape(u3.shape)
    prev1 = jnp.where(row == 0, hist1, prev1)
    prev2 = jnp.where(row == 0, hist0, jnp.where(row == 1, hist1, prev2))
    y = wdw[0:1, :] * prev2 + wdw[1:2, :] * prev1 + wdw[2:3, :] * u3
    z = (bg.reshape(u3.shape) * y).astype(BF16).reshape(tm, CONV_TILE)
    return z, u3


def _conv_in_prompt_kernel(xn_ref, wb_ref, wc_ref, wx_ref, wdw_ref, z_ref, st_ref, w_ref, carry_ref,
                           *, tiles_per_seq):
    tm = xn_ref.shape[0]
    _cast_conv_weights(wb_ref, wc_ref, wx_ref, w_ref)

    @pl.when(pl.program_id(1) % tiles_per_seq == 0)
    def _():
        carry_ref[...] = jnp.zeros_like(carry_ref)

    h = jnp.dot(xn_ref[...], w_ref[...], preferred_element_type=F32)
    hist0 = carry_ref[0:1, :].reshape(1, 1, CONV_TILE)
    hist1 = carry_ref[1:2, :].reshape(1, 1, CONV_TILE)
    z, u3 = _gated_conv(h, hist0, hist1, wdw_ref[...], tm)
    z_ref[...] = z
    last = u3[0, tm - 2:, :]
    carry_ref[0:2, :] = last
    st_ref[0] = last


def _conv_weight_specs(d, layer):
    nj = d // CONV_TILE
    part = lambda p: pl.BlockSpec((None, d, CONV_TILE), lambda j, i: (layer, 0, p * nj + j),
                                  pipeline_mode=pl.Buffered(1))
    return [part(0), part(1), part(2), pl.BlockSpec((None, 3, CONV_TILE), lambda j, i: (layer, 0, j))]


def _conv_in_prompt(xn, w_in, w_dw, layer, batch, seq):
    t, d = xn.shape
    tiles_per_seq = seq // TM_IN
    return pl.pallas_call(
        functools.partial(_conv_in_prompt_kernel, tiles_per_seq=tiles_per_seq),
        out_shape=[jax.ShapeDtypeStruct((t, d), BF16), jax.ShapeDtypeStruct((batch, 2, d), F32)],
        grid=(d // CONV_TILE, t // TM_IN),
        in_specs=[pl.BlockSpec((TM_IN, d), lambda j, i: (i, 0))] + _conv_weight_specs(d, layer),
        out_specs=[
            pl.BlockSpec((TM_IN, CONV_TILE), lambda j, i: (i, j)),
            pl.BlockSpec((1, 2, CONV_TILE), lambda j, i: (i // tiles_per_seq, 0, j)),
        ],
        scratch_shapes=[pltpu.VMEM((d, 3 * CONV_TILE), BF16), pltpu.VMEM((8, CONV_TILE), F32)],
        compiler_params=_params(("arbitrary", "arbitrary")),
        name="conv_in_prompt",
    )(xn, w_in, w_in, w_in, w_dw)


def _conv_in_sample_kernel(xn_ref, wb_ref, wc_ref, wx_ref, wdw_ref, hist_ref, z_ref, st_ref, w_ref, *, seq):
    _cast_conv_weights(wb_ref, wc_ref, wx_ref, w_ref)
    h = jnp.dot(xn_ref[...], w_ref[...], preferred_element_type=F32)
    hist = hist_ref[...]
    z, u3 = _gated_conv(h, hist[:, 0:1, :], hist[:, 1:2, :], wdw_ref[...], seq)
    z_ref[...] = z
    st_ref[...] = u3[:, seq - 2:, :]


def _conv_in_sample(xn, w_in, w_dw, layer, hist, seq):
    t, d = xn.shape
    nseq = TM_IN // seq
    hist_spec = pl.BlockSpec((nseq, 2, CONV_TILE), lambda j, i: (i, 0, j))
    return pl.pallas_call(
        functools.partial(_conv_in_sample_kernel, seq=seq),
        out_shape=[jax.ShapeDtypeStruct((t, d), BF16), jax.ShapeDtypeStruct(hist.shape, F32)],
        grid=(d // CONV_TILE, t // TM_IN),
        in_specs=[pl.BlockSpec((TM_IN, d), lambda j, i: (i, 0))] + _conv_weight_specs(d, layer) + [hist_spec],
        out_specs=[pl.BlockSpec((TM_IN, CONV_TILE), lambda j, i: (i, j)), hist_spec],
        scratch_shapes=[pltpu.VMEM((d, 3 * CONV_TILE), BF16)],
        compiler_params=_params(("arbitrary", "arbitrary")),
        name="conv_in_sample",
    )(xn, w_in, w_in, w_in, w_dw, hist)


def _qk(q, k):
    return lax.dot_general(q, k, (((1,), (1,)), ((), ())), preferred_element_type=F32)


def _band_attn_prompt_kernel(q_ref, k_ref, v_ref, bias_ref, o_ref):
    seq = q_ref.shape[0]
    sm_scale = HEAD_DIM ** -0.5
    for m in range(seq // QUERY_GROUP):
        q0 = m * QUERY_GROUP
        lo = max(0, q0 - PAST_WIN)
        hi = q0 + QUERY_GROUP
        col0 = lo - (q0 - PAST_WIN)
        s = _qk(q_ref[q0:hi, :], k_ref[lo:hi, :]) * sm_scale + bias_ref[:, col0:col0 + hi - lo]
        p = jnp.exp(s - jnp.max(s, axis=-1, keepdims=True))
        l = jnp.sum(p, axis=-1, keepdims=True)
        o = jnp.dot(p.astype(BF16), v_ref[lo:hi, :], preferred_element_type=F32) / l
        o_ref[q0:hi, :] = o.astype(BF16)


def _band_attn_prompt(qkv_hm, bias, batch, seq):
    head = lambda off: pl.BlockSpec((None, seq, HEAD_DIM), lambda h, b: (off + h, b, 0))
    return pl.pallas_call(
        _band_attn_prompt_kernel,
        out_shape=jax.ShapeDtypeStruct((N_HEADS, batch * seq, HEAD_DIM), BF16),
        grid=(N_HEADS, batch),
        in_specs=[head(0), head(N_HEADS), head(2 * N_HEADS),
                  pl.BlockSpec((None,) + bias.shape[1:], lambda h, b: (h, 0, 0))],
        out_specs=head(0),
        compiler_params=_params(("arbitrary", "arbitrary")),
        name="band_attn_prompt",
    )(qkv_hm, qkv_hm, qkv_hm, bias)


def _band_attn_sample_kernel(qkv_ref, ck_ref, cv_ref, bias_c_ref, bias_n_ref, o_ref):
    sm_scale = HEAD_DIM ** -0.5
    for h in range(N_HEADS):
        c = slice(h * HEAD_DIM, (h + 1) * HEAD_DIM)
        q, kn, vn = qkv_ref[h], qkv_ref[N_HEADS + h], qkv_ref[2 * N_HEADS + h]
        kc = ck_ref[:, c].astype(BF16)
        vc = cv_ref[:, c].astype(BF16)
        s_c = _qk(q, kc) * sm_scale + bias_c_ref[h]
        s_n = _qk(q, kn) * sm_scale + bias_n_ref[h]
        mx = jnp.maximum(jnp.max(s_c, axis=-1, keepdims=True), jnp.max(s_n, axis=-1, keepdims=True))
        p_c = jnp.exp(s_c - mx)
        p_n = jnp.exp(s_n - mx)
        l = jnp.sum(p_c, axis=-1, keepdims=True) + jnp.sum(p_n, axis=-1, keepdims=True)
        o = (jnp.dot(p_c.astype(BF16), vc, preferred_element_type=F32)
             + jnp.dot(p_n.astype(BF16), vn, preferred_element_type=F32)) / l
        o_ref[h] = o.astype(BF16)


def _band_attn_sample(qkv_hm, ck, cv, bias_c, bias_n, batch, seq):
    pw = ck.shape[0] // batch
    const = lambda a: pl.BlockSpec(a.shape, lambda b: (0,) * a.ndim, pipeline_mode=pl.Buffered(1))
    return pl.pallas_call(
        _band_attn_sample_kernel,
        out_shape=jax.ShapeDtypeStruct((N_HEADS, batch * seq, HEAD_DIM), BF16),
        grid=(batch,),
        in_specs=[
            pl.BlockSpec((3 * N_HEADS, seq, HEAD_DIM), lambda b: (0, b, 0)),
            pl.BlockSpec((pw, D_MODEL), lambda b: (b, 0)),
            pl.BlockSpec((pw, D_MODEL), lambda b: (b, 0)),
            const(bias_c), const(bias_n),
        ],
        out_specs=pl.BlockSpec((N_HEADS, seq, HEAD_DIM), lambda b: (0, b, 0)),
        compiler_params=_params(("arbitrary",)),
        name="band_attn_sample",
    )(qkv_hm, ck, cv, bias_c, bias_n)


def _band_bias_table(rel_bias):
    nh = rel_bias.shape[0]
    n_keys = PAST_WIN + QUERY_GROUP
    period = n_keys + QUERY_GROUP
    n_far_past = PAST_WIN - MAX_REL + QUERY_GROUP - 1
    n_future = period - n_far_past - rel_bias.shape[1]
    f = jnp.concatenate([
        jnp.broadcast_to(rel_bias[:, -1:], (nh, n_far_past)),
        rel_bias[:, ::-1],
        jnp.broadcast_to(rel_bias[:, :1], (nh, n_future)),
    ], axis=1)
    f = jnp.roll(f, -(QUERY_GROUP - 1), axis=1)
    flat = jnp.tile(f, (1, QUERY_GROUP))[:, :QUERY_GROUP * (period - 1)]
    toeplitz = flat.reshape(nh, QUERY_GROUP, period - 1)[:, :, :n_keys]
    qi = jnp.arange(QUERY_GROUP)[:, None]
    kj = jnp.arange(n_keys)[None, :]
    band_lo = (qi // CHUNK) * CHUNK
    in_band = (kj >= band_lo) & (kj < band_lo + BAND)
    return jnp.where(in_band[None], toeplitz, NEG_INF).astype(F32)


def _mem_attn_kernel(xn_ref, wq_ref, mk_ref, mv_ref, wo_ref, x_ref, gp_ref, gn_ref,
                     xo_ref, xno_ref, acc0_ref, acc1_ref):
    sm_scale = MEM_HEAD_DIM ** -0.5

    def compute():
        q = jnp.dot(xn_ref[...], wq_ref[...], preferred_element_type=F32)
        heads = []
        for h in range(MEM_HEADS):
            c = slice(h * MEM_HEAD_DIM, (h + 1) * MEM_HEAD_DIM)
            s = _qk(q[:, c].astype(BF16), mk_ref[:, c].astype(BF16)) * sm_scale
            p = jnp.exp(s - jnp.max(s, axis=-1, keepdims=True))
            l = jnp.sum(p, axis=-1, keepdims=True)
            o = jnp.dot(p.astype(BF16), mv_ref[:, c].astype(BF16), preferred_element_type=F32) / l
            heads.append(o.astype(BF16))
        return jnp.dot(jnp.concatenate(heads, axis=1), wo_ref[...], preferred_element_type=F32)

    _skewed_step(compute, x_ref, gp_ref, gn_ref, xo_ref, xno_ref, acc0_ref, acc1_ref, 1.0)


def _mem_attn(xn, x, mk, mv, wq, wo, layer, g4, batch, tm):
    t, d = x.shape
    n = t // tm
    tiles_per_seq = n // batch
    cur = lambda i: (jnp.minimum(i, n - 1), 0)
    prev = lambda i: (jnp.maximum(i - 1, 0), 0)
    mem = lambda i: (jnp.minimum(i, n - 1) // tiles_per_seq, 0)
    return pl.pallas_call(
        _mem_attn_kernel,
        out_shape=[jax.ShapeDtypeStruct((t, d), F32), jax.ShapeDtypeStruct((t, d), BF16)],
        grid=(n + 1,),
        in_specs=[
            pl.BlockSpec((tm, d), cur),
            _layer_spec((d, MEM_WIDTH), layer),
            pl.BlockSpec((N_MEM, MEM_WIDTH), mem),
            pl.BlockSpec((N_MEM, MEM_WIDTH), mem),
            _layer_spec((MEM_WIDTH, d), layer),
            pl.BlockSpec((tm, d), prev),
            _gain_spec(layer, 5),
            _gain_spec(layer, 6),
        ],
        out_specs=[pl.BlockSpec((tm, d), prev), pl.BlockSpec((tm, d), prev)],
        scratch_shapes=[pltpu.VMEM((tm, d), F32), pltpu.VMEM((tm, d), F32)],
        compiler_params=_params(("arbitrary",)),
        name="mem_attn",
    )(xn, wq, mk, mv, wo, x, g4, g4)


def _mem_project_kernel(mem_ref, g_ref, wf_ref, k_ref, v_ref, w_ref):
    @pl.when(pl.program_id(1) == 0)
    def _():
        w_ref[...] = wf_ref[...].astype(BF16)

    kv = jnp.dot(_rms(mem_ref[...], g_ref[...]).astype(BF16), w_ref[...], preferred_element_type=F32)
    k_ref[...] = kv[:, :MEM_WIDTH]
    v_ref[...] = kv[:, MEM_WIDTH:]


def _mem_project(mem, g_mem, w_kv, tm):
    t, d = mem.shape
    depth = w_kv.shape[0]
    out = jax.ShapeDtypeStruct((depth, t, MEM_WIDTH), F32)
    out_spec = pl.BlockSpec((None, tm, MEM_WIDTH), lambda l, i: (l, i, 0))
    return pl.pallas_call(
        _mem_project_kernel,
        out_shape=[out, out],
        grid=(depth, t // tm),
        in_specs=[
            pl.BlockSpec((tm, d), lambda l, i: (i, 0)),
            pl.BlockSpec((None, 1, d), lambda l, i: (l, 0, 0)),
            pl.BlockSpec((None, d, 2 * MEM_WIDTH), lambda l, i: (l, 0, 0)),
        ],
        out_specs=[out_spec, out_spec],
        scratch_shapes=[pltpu.VMEM((d, 2 * MEM_WIDTH), BF16)],
        compiler_params=_params(("arbitrary", "arbitrary")),
        name="mem_project",
    )(mem, g_mem, w_kv)


def _trunk(x, mem_k, mem_v, weights, *, batch, seq, conv_hist, band_cache, bias_tables):
    prompt = conv_hist is None
    (g4, w_ffn1_in, w_ffn1_out, w_ffn2_in, w_ffn2_out, w_conv_in, w_conv_dw, w_conv_out,
     w_attn_qkv, w_attn_o, w_mem_q, w_mem_o) = weights
    depth = g4.shape[0]
    tm_mem = TM_MEM if prompt else seq

    xn = _rms_cast(x, g4)
    conv_state = band_kv = None
    for i in range(depth):
        m = i // 2
        a = _ffn_in(xn, w_ffn1_in, i)
        x, xn = _mm_res_norm(a, w_ffn1_out, i, x, g4, (i, 1), (i, 2), 0.5, TM_FFN_OUT)
        if i % 2 == 0:
            if prompt:
                z, conv_state = _conv_in_prompt(xn, w_conv_in, w_conv_dw, m, batch, seq)
            else:
                z, conv_state = _conv_in_sample(xn, w_conv_in, w_conv_dw, m, conv_hist, seq)
            x, xn = _mm_res_norm(z, w_conv_out, m, x, g4, (i, 3), (i, 4), 1.0, TM_MIX_OUT)
        else:
            if prompt:
                qkv_hm, band_kv = _qkv(xn, w_attn_qkv, m, min(PAST_WIN, seq), seq // TM_IN)
                o = _band_attn_prompt(qkv_hm, bias_tables[0], batch, seq)
            else:
                qkv_hm, band_kv = _qkv(xn, w_attn_qkv, m, TM_IN, 1)
                o = _band_attn_sample(qkv_hm, band_cache[0], band_cache[1], bias_tables[1], bias_tables[2],
                                      batch, seq)
            x, xn = _mm_res_norm(o, w_attn_o, m, x, g4, (i, 3), (i, 4), 1.0, TM_MIX_OUT)
        x, xn = _mem_attn(xn, x, mem_k[i], mem_v[i], w_mem_q, w_mem_o, i, g4, batch, tm_mem)
        a = _ffn_in(xn, w_ffn2_in, i)
        last = i == depth - 1
        x, xn = _mm_res_norm(a, w_ffn2_out, i, x, g4, (i, 7), ((i + 1) % depth, 0), 0.5, TM_FFN_OUT,
                             emit_xn=not last)
    return x, conv_state, band_kv


def kernel(x_prompt, x_sample, state_conv, cache_band_k, cache_band_v, cache_mem_k, cache_mem_v, mem_prompt, g_norm, g_mem, w_ffn1_in, w_ffn1_out, w_ffn2_in, w_ffn2_out, w_conv_in, w_conv_dw, w_conv_out, w_attn_qkv, rel_bias, w_attn_o, w_mem_q, w_mem_kv, w_mem_o):
    batch, seq, d = x_prompt.shape
    dec_batch, dec_seq, _ = x_sample.shape
    depth = g_norm.shape[0]
    assert depth == 2 and state_conv.shape[0] == 1 and cache_band_k.shape[0] == 1
    assert cache_band_k.shape[2] == PAST_WIN and dec_seq == CHUNK and seq % QUERY_GROUP == 0
    assert seq % TM_IN == 0 and seq >= PAST_WIN and TM_IN % dec_seq == 0

    weights = (
        g_norm[:, :, None, :],
        w_ffn1_in, w_ffn1_out.astype(BF16), w_ffn2_in, w_ffn2_out.astype(BF16),
        w_conv_in, w_conv_dw, w_conv_out.astype(BF16),
        w_attn_qkv, w_attn_o.astype(BF16), w_mem_q.astype(BF16), w_mem_o.astype(BF16),
    )
    table = _band_bias_table(rel_bias[0])
    bias_tables = (table, table[:, :CHUNK, :PAST_WIN], table[:, :CHUNK, PAST_WIN:BAND])

    mem_k_p, mem_v_p = _mem_project(mem_prompt.reshape(batch * N_MEM, d), g_mem[:, None, :], w_mem_kv, 512)

    y_p, conv_p, kv_p = _trunk(
        x_prompt.reshape(batch * seq, d), mem_k_p, mem_v_p, weights,
        batch=batch, seq=seq, conv_hist=None, band_cache=None, bias_tables=bias_tables)

    y_s, conv_s, kv_s = _trunk(
        x_sample.reshape(dec_batch * dec_seq, d),
        cache_mem_k.reshape(depth, dec_batch * N_MEM, MEM_WIDTH),
        cache_mem_v.reshape(depth, dec_batch * N_MEM, MEM_WIDTH), weights,
        batch=dec_batch, seq=dec_seq, conv_hist=state_conv[0],
        band_cache=(cache_band_k[0].reshape(dec_batch * PAST_WIN, d),
                    cache_band_v[0].reshape(dec_batch * PAST_WIN, d)),
        bias_tables=bias_tables)

    keep = min(PAST_WIN, seq)
    kv_s = kv_s.reshape(dec_batch, dec_seq, 2 * d)
    mem_shape = (depth, batch, N_MEM, MEM_HEADS, MEM_HEAD_DIM)
    return (y_p.reshape(batch, seq, d), y_s.reshape(dec_batch, dec_seq, d),
            conv_p[None],
            kv_p[:, :, :d].reshape(1, batch, keep, N_HEADS, HEAD_DIM),
            kv_p[:, :, d:].reshape(1, batch, keep, N_HEADS, HEAD_DIM),
            mem_k_p.reshape(mem_shape), mem_v_p.reshape(mem_shape),
            conv_s[None],
            kv_s[:, :, :d].reshape(1, dec_batch, dec_seq, N_HEADS, HEAD_DIM),
            kv_s[:, :, d:].reshape(1, dec_batch, dec_seq, N_HEADS, HEAD_DIM))
```

```python
import functools

import jax
import jax.numpy as jnp
from jax import lax
from jax.experimental import pallas as pl
from jax.experimental.pallas import tpu as pltpu

F32 = jnp.float32
BF16 = jnp.bfloat16

D_MODEL = 2048
D_FF = 5632
N_HEADS = 16
HEAD_DIM = 128
CHUNK = 64
PAST_WIN = 512
BAND = PAST_WIN + CHUNK
MAX_REL = 128
N_MEM = 256
MEM_HEADS = 4
MEM_HEAD_DIM = 128
MEM_WIDTH = MEM_HEADS * MEM_HEAD_DIM
EPS = 1e-6
NEG_INF = -1e30

V7X_VMEM_LIMIT_BYTES = 58 * 1024 * 1024
BF16_SUBLANES = 16
FF_TILE = 512
CONV_TILE = 512
TM_QKV = PAST_WIN
QUERY_GROUP = 4 * CHUNK
TM_IN = 1024
TM_FFN_OUT = 256
TM_MIX_OUT = 512
TM_MEM = 512


def _params(sem):
    return pltpu.CompilerParams(dimension_semantics=sem, vmem_limit_bytes=V7X_VMEM_LIMIT_BYTES)


def _layer_spec(shape, layer):
    zeros = (0,) * len(shape)
    return pl.BlockSpec((None,) + tuple(shape), lambda *_: (layer,) + zeros, pipeline_mode=pl.Buffered(1))


def _gain_spec(layer, n):
    return pl.BlockSpec((None, None, 1, D_MODEL), lambda *_: (layer, n, 0, 0), pipeline_mode=pl.Buffered(1))


def _rms(x, g):
    return x * lax.rsqrt(jnp.mean(x * x, axis=-1, keepdims=True) + EPS) * g


def _rms_cast_kernel(x_ref, g_ref, o_ref):
    o_ref[...] = _rms(x_ref[...], g_ref[...]).astype(BF16)


def _rms_cast(x, g4):
    t, d = x.shape
    return pl.pallas_call(
        _rms_cast_kernel,
        out_shape=jax.ShapeDtypeStruct((t, d), BF16),
        grid=(t // TM_IN,),
        in_specs=[pl.BlockSpec((TM_IN, d), lambda i: (i, 0)), _gain_spec(0, 0)],
        out_specs=pl.BlockSpec((TM_IN, d), lambda i: (i, 0)),
        compiler_params=_params(("arbitrary",)),
        name="rms_cast",
    )(x, g4)


def _res_norm_epilogue(acc_ref, x_ref, gp, gn, xo_ref, xno_ref):
    rows = BF16_SUBLANES
    for r in range(acc_ref.shape[0] // rows):
        sl = slice(r * rows, (r + 1) * rows)
        xnew = x_ref[sl, :] + _rms(acc_ref[sl, :], gp)
        xo_ref[sl, :] = xnew
        if xno_ref is not None:
            xno_ref[sl, :] = _rms(xnew, gn).astype(BF16)


def _skewed_step(compute_tile, x_ref, gp_ref, gn_ref, xo_ref, xno_ref, acc0_ref, acc1_ref, scale):
    i = pl.program_id(0)

    @pl.when(i == 0)
    def _():
        acc1_ref[...] = jnp.zeros_like(acc1_ref)

    def step(prev_ref, cur_ref):
        gp = gp_ref[...] * scale
        gn = gn_ref[...]
        _res_norm_epilogue(prev_ref, x_ref, gp, gn, xo_ref, xno_ref)
        cur_ref[...] = compute_tile()

    @pl.when(i % 2 == 0)
    def _():
        step(acc1_ref, acc0_ref)

    @pl.when(i % 2 == 1)
    def _():
        step(acc0_ref, acc1_ref)


def _load_rows(a_ref):
    if len(a_ref.shape) == 2:
        return a_ref[...]
    return jnp.concatenate([a_ref[h] for h in range(a_ref.shape[0])], axis=1)


def _mm_res_norm_kernel(a_ref, w_ref, x_ref, gp_ref, gn_ref, xo_ref, *rest, scale, emit_xn):
    xno_ref, acc0_ref, acc1_ref = rest if emit_xn else (None,) + rest
    compute = lambda: jnp.dot(_load_rows(a_ref), w_ref[...], preferred_element_type=F32)
    _skewed_step(compute, x_ref, gp_ref, gn_ref, xo_ref, xno_ref, acc0_ref, acc1_ref, scale)


def _mm_res_norm(a, w, w_layer, x, g4, post_gain, next_gain, scale, tm, emit_xn=True):
    t, d = x.shape
    k = w.shape[1]
    n = t // tm
    cur = lambda i: jnp.minimum(i, n - 1)
    prev = lambda i: (jnp.maximum(i - 1, 0), 0)
    if a.ndim == 2:
        a_spec = pl.BlockSpec((tm, k), lambda i: (cur(i), 0))
    else:
        a_spec = pl.BlockSpec((a.shape[0], tm, a.shape[2]), lambda i: (0, cur(i), 0))
    out_shape = [jax.ShapeDtypeStruct((t, d), F32)]
    out_specs = [pl.BlockSpec((tm, d), prev)]
    if emit_xn:
        out_shape.append(jax.ShapeDtypeStruct((t, d), BF16))
        out_specs.append(pl.BlockSpec((tm, d), prev))
    outs = pl.pallas_call(
        functools.partial(_mm_res_norm_kernel, scale=scale, emit_xn=emit_xn),
        out_shape=out_shape,
        grid=(n + 1,),
        in_specs=[
            a_spec,
            _layer_spec((k, d), w_layer),
            pl.BlockSpec((tm, d), prev),
            _gain_spec(*post_gain),
            _gain_spec(*next_gain),
        ],
        out_specs=out_specs,
        scratch_shapes=[pltpu.VMEM((tm, d), F32), pltpu.VMEM((tm, d), F32)],
        compiler_params=_params(("arbitrary",)),
        name="mm_res_norm",
    )(a, w, x, g4, g4)
    return (outs[0], outs[1]) if emit_xn else (outs[0], None)


def _ffn_in_kernel(xn_ref, wg_ref, wu_ref, o_ref, w_ref):
    @pl.when(pl.program_id(1) == 0)
    def _():
        w_ref[:, :FF_TILE] = wg_ref[...].astype(BF16)
        w_ref[:, FF_TILE:] = wu_ref[...].astype(BF16)

    h = jnp.dot(xn_ref[...], w_ref[...], preferred_element_type=F32)
    o_ref[...] = (jax.nn.silu(h[:, :FF_TILE]) * h[:, FF_TILE:]).astype(BF16)


def _ffn_in(xn, w_in, layer):
    t, d = xn.shape
    nj = D_FF // FF_TILE
    return pl.pallas_call(
        _ffn_in_kernel,
        out_shape=jax.ShapeDtypeStruct((t, D_FF), BF16),
        grid=(nj, t // TM_IN),
        in_specs=[
            pl.BlockSpec((TM_IN, d), lambda j, i: (i, 0)),
            pl.BlockSpec((None, d, FF_TILE), lambda j, i: (layer, 0, j)),
            pl.BlockSpec((None, d, FF_TILE), lambda j, i: (layer, 0, nj + j)),
        ],
        out_specs=pl.BlockSpec((TM_IN, FF_TILE), lambda j, i: (i, j)),
        scratch_shapes=[pltpu.VMEM((d, 2 * FF_TILE), BF16)],
        compiler_params=_params(("arbitrary", "arbitrary")),
        name="ffn_in",
    )(xn, w_in, w_in)


def _qkv_kernel(xn_ref, w_ref, hm_ref, k_ref, v_ref, *, keep_every):
    j, i = pl.program_id(0), pl.program_id(1)
    acc = jnp.dot(xn_ref[...], w_ref[...], preferred_element_type=F32)
    heads = [acc[:, h * HEAD_DIM:(h + 1) * HEAD_DIM] for h in range(N_HEADS)]
    for h in range(N_HEADS):
        hm_ref[h] = heads[h].astype(BF16)

    def keep(ref):
        for h in range(N_HEADS):
            ref[pl.ds(h, acc.shape[0], stride=N_HEADS), :] = heads[h]

    kept_tile = i % keep_every == keep_every - 1

    @pl.when((j == 1) & kept_tile)
    def _():
        keep(k_ref)

    @pl.when((j == 2) & kept_tile)
    def _():
        keep(v_ref)


def _qkv(xn, w_qkv, layer, keep_every):
    t, d = xn.shape
    n_keep = t // (TM_QKV * keep_every)
    kept = jax.ShapeDtypeStruct((n_keep, TM_QKV * N_HEADS, HEAD_DIM), F32)

    def k_index(j, i):
        return (jnp.where(j < 1, 0, jnp.where(j > 1, n_keep - 1, i // keep_every)), 0, 0)

    def v_index(j, i):
        return (jnp.where(j < 2, 0, i // keep_every), 0, 0)

    return pl.pallas_call(
        functools.partial(_qkv_kernel, keep_every=keep_every),
        out_shape=[jax.ShapeDtypeStruct((3 * N_HEADS, t, HEAD_DIM), BF16), kept, kept],
        grid=(3, t // TM_QKV),
        in_specs=[
            pl.BlockSpec((TM_QKV, d), lambda j, i: (i, 0)),
            pl.BlockSpec((None, d, D_MODEL), lambda j, i: (layer, 0, j), pipeline_mode=pl.Buffered(1)),
        ],
        out_specs=[
            pl.BlockSpec((N_HEADS, TM_QKV, HEAD_DIM), lambda j, i: (j, i, 0)),
            pl.BlockSpec((None, TM_QKV * N_HEADS, HEAD_DIM), k_index),
            pl.BlockSpec((None, TM_QKV * N_HEADS, HEAD_DIM), v_index),
        ],
        compiler_params=_params(("arbitrary", "arbitrary")),
        name="qkv",
    )(xn, w_qkv)


def _cast_conv_weights(wb_ref, wc_ref, wx_ref, w_ref):
    @pl.when(pl.program_id(1) == 0)
    def _():
        w_ref[:, :CONV_TILE] = wb_ref[...].astype(BF16)
        w_ref[:, CONV_TILE:2 * CONV_TILE] = wc_ref[...].astype(BF16)
        w_ref[:, 2 * CONV_TILE:] = wx_ref[...].astype(BF16)


def _gated_conv(h, hist0, hist1, wdw, seq):
    tm = h.shape[0]
    nseq = tm // seq
    bg = h[:, :CONV_TILE]
    u = h[:, CONV_TILE:2 * CONV_TILE] * h[:, 2 * CONV_TILE:]
    u3 = u.reshape(nseq, seq, CONV_TILE)
    row = lax.broadcasted_iota(jnp.int32, u3.shape, 1)
    prev1 = pltpu.roll(u, 1, axis=0).reshape(u3.shape)
    prev2 = pltpu.roll(u, 2, axis=0).reshape(u3.shape)
    prev1 = jnp.where(row == 0, hist1, prev1)
    prev2 = jnp.where(row == 0, hist0, jnp.where(row == 1, hist1, prev2))
    y = wdw[0:1, :] * prev2 + wdw[1:2, :] * prev1 + wdw[2:3, :] * u3
    z = (bg.reshape(u3.shape) * y).astype(BF16).reshape(tm, CONV_TILE)
    return z, u3


def _conv_in_prompt_kernel(xn_ref, wb_ref, wc_ref, wx_ref, wdw_ref, z_ref, st_ref, w_ref, carry_ref,
                           *, tiles_per_seq):
    tm = xn_ref.shape[0]
    _cast_conv_weights(wb_ref, wc_ref, wx_ref, w_ref)

    @pl.when(pl.program_id(1) % tiles_per_seq == 0)
    def _():
        carry_ref[...] = jnp.zeros_like(carry_ref)

    h = jnp.dot(xn_ref[...], w_ref[...], preferred_element_type=F32)
    hist0 = carry_ref[0:1, :].reshape(1, 1, CONV_TILE)
    hist1 = carry_ref[1:2, :].reshape(1, 1, CONV_TILE)
    z, u3 = _gated_conv(h, hist0, hist1, wdw_ref[...], tm)
    z_ref[...] = z
    last = u3[0, tm - 2:, :]
    carry_ref[0:2, :] = last
    st_ref[0] = last


def _conv_weight_specs(d, layer):
    nj = d // CONV_TILE
    part = lambda p: pl.BlockSpec((None, d, CONV_TILE), lambda j, i: (layer, 0, p * nj + j),
                                  pipeline_mode=pl.Buffered(1))
    return [part(0), part(1), part(2), pl.BlockSpec((None, 3, CONV_TILE), lambda j, i: (layer, 0, j))]


def _conv_in_prompt(xn, w_in, w_dw, layer, batch, seq):
    t, d = xn.shape
    tiles_per_seq = seq // TM_IN
    return pl.pallas_call(
        functools.partial(_conv_in_prompt_kernel, tiles_per_seq=tiles_per_seq),
        out_shape=[jax.ShapeDtypeStruct((t, d), BF16), jax.ShapeDtypeStruct((batch, 2, d), F32)],
        grid=(d // CONV_TILE, t // TM_IN),
        in_specs=[pl.BlockSpec((TM_IN, d), lambda j, i: (i, 0))] + _conv_weight_specs(d, layer),
        out_specs=[
            pl.BlockSpec((TM_IN, CONV_TILE), lambda j, i: (i, j)),
            pl.BlockSpec((1, 2, CONV_TILE), lambda j, i: (i // tiles_per_seq, 0, j)),
        ],
        scratch_shapes=[pltpu.VMEM((d, 3 * CONV_TILE), BF16), pltpu.VMEM((8, CONV_TILE), F32)],
        compiler_params=_params(("arbitrary", "arbitrary")),
        name="conv_in_prompt",
    )(xn, w_in, w_in, w_in, w_dw)


def _conv_in_sample_kernel(xn_ref, wb_ref, wc_ref, wx_ref, wdw_ref, hist_ref, z_ref, st_ref, w_ref, *, seq):
    _cast_conv_weights(wb_ref, wc_ref, wx_ref, w_ref)
    h = jnp.dot(xn_ref[...], w_ref[...], preferred_element_type=F32)
    hist = hist_ref[...]
    z, u3 = _gated_conv(h, hist[:, 0:1, :], hist[:, 1:2, :], wdw_ref[...], seq)
    z_ref[...] = z
    st_ref[...] = u3[:, seq - 2:, :]


def _conv_in_sample(xn, w_in, w_dw, layer, hist, seq):
    t, d = xn.shape
    nseq = TM_IN // seq
    hist_spec = pl.BlockSpec((nseq, 2, CONV_TILE), lambda j, i: (i, 0, j))
    return pl.pallas_call(
        functools.partial(_conv_in_sample_kernel, seq=seq),
        out_shape=[jax.ShapeDtypeStruct((t, d), BF16), jax.ShapeDtypeStruct(hist.shape, F32)],
        grid=(d // CONV_TILE, t // TM_IN),
        in_specs=[pl.BlockSpec((TM_IN, d), lambda j, i: (i, 0))] + _conv_weight_specs(d, layer) + [hist_spec],
        out_specs=[pl.BlockSpec((TM_IN, CONV_TILE), lambda j, i: (i, j)), hist_spec],
        scratch_shapes=[pltpu.VMEM((d, 3 * CONV_TILE), BF16)],
        compiler_params=_params(("arbitrary", "arbitrary")),
        name="conv_in_sample",
    )(xn, w_in, w_in, w_in, w_dw, hist)


def _qk(q, k):
    return lax.dot_general(q, k, (((1,), (1,)), ((), ())), preferred_element_type=F32)


def _band_attn_prompt_kernel(q_ref, k_ref, v_ref, bias_ref, o_ref):
    seq = q_ref.shape[0]
    sm_scale = HEAD_DIM ** -0.5
    n_groups = seq // QUERY_GROUP

    def key_range(m):
        q0 = m * QUERY_GROUP
        return max(0, q0 - PAST_WIN), q0 + QUERY_GROUP

    def scores(m):
        q0 = m * QUERY_GROUP
        lo, hi = key_range(m)
        col0 = lo - (q0 - PAST_WIN)
        return _qk(q_ref[q0:hi, :], k_ref[lo:hi, :]) * sm_scale + bias_ref[:, col0:col0 + hi - lo]

    s_next = scores(0)
    for m in range(n_groups):
        s = s_next
        if m + 1 < n_groups:
            s_next = scores(m + 1)
        lo, hi = key_range(m)
        p = jnp.exp(s - jnp.max(s, axis=-1, keepdims=True))
        l = jnp.sum(p, axis=-1, keepdims=True)
        o = jnp.dot(p.astype(BF16), v_ref[lo:hi, :], preferred_element_type=F32) / l
        o_ref[m * QUERY_GROUP:hi, :] = o.astype(BF16)


def _band_attn_prompt(qkv_hm, bias, batch, seq):
    head = lambda off: pl.BlockSpec((None, seq, HEAD_DIM), lambda h, b: (off + h, b, 0))
    return pl.pallas_call(
        _band_attn_prompt_kernel,
        out_shape=jax.ShapeDtypeStruct((N_HEADS, batch * seq, HEAD_DIM), BF16),
        grid=(N_HEADS, batch),
        in_specs=[head(0), head(N_HEADS), head(2 * N_HEADS),
                  pl.BlockSpec((None,) + bias.shape[1:], lambda h, b: (h, 0, 0))],
        out_specs=head(0),
        compiler_params=_params(("arbitrary", "arbitrary")),
        name="band_attn_prompt",
    )(qkv_hm, qkv_hm, qkv_hm, bias)


def _band_attn_sample_kernel(qkv_ref, ck_ref, cv_ref, bias_c_ref, bias_n_ref, o_ref):
    sm_scale = HEAD_DIM ** -0.5
    for h in range(N_HEADS):
        q, kn, vn = qkv_ref[h], qkv_ref[N_HEADS + h], qkv_ref[2 * N_HEADS + h]
        head_rows = pl.ds(h, PAST_WIN, stride=N_HEADS)
        kc = ck_ref[head_rows, :].astype(BF16)
        vc = cv_ref[head_rows, :].astype(BF16)
        s_c = _qk(q, kc) * sm_scale + bias_c_ref[h]
        s_n = _qk(q, kn) * sm_scale + bias_n_ref[h]
        mx = jnp.maximum(jnp.max(s_c, axis=-1, keepdims=True), jnp.max(s_n, axis=-1, keepdims=True))
        p_c = jnp.exp(s_c - mx)
        p_n = jnp.exp(s_n - mx)
        l = jnp.sum(p_c, axis=-1, keepdims=True) + jnp.sum(p_n, axis=-1, keepdims=True)
        o = (jnp.dot(p_c.astype(BF16), vc, preferred_element_type=F32)
             + jnp.dot(p_n.astype(BF16), vn, preferred_element_type=F32)) / l
        o_ref[h] = o.astype(BF16)


def _band_attn_sample(qkv_hm, ck, cv, bias_c, bias_n, batch, seq):
    cache = pl.BlockSpec((None, PAST_WIN * N_HEADS, HEAD_DIM), lambda b: (b, 0, 0))
    const = lambda a: pl.BlockSpec(a.shape, lambda b: (0,) * a.ndim, pipeline_mode=pl.Buffered(1))
    return pl.pallas_call(
        _band_attn_sample_kernel,
        out_shape=jax.ShapeDtypeStruct((N_HEADS, batch * seq, HEAD_DIM), BF16),
        grid=(batch,),
        in_specs=[
            pl.BlockSpec((3 * N_HEADS, seq, HEAD_DIM), lambda b: (0, b, 0)),
            cache, cache,
            const(bias_c), const(bias_n),
        ],
        out_specs=pl.BlockSpec((N_HEADS, seq, HEAD_DIM), lambda b: (0, b, 0)),
        compiler_params=_params(("arbitrary",)),
        name="band_attn_sample",
    )(qkv_hm, ck, cv, bias_c, bias_n)


def _band_bias_table(rel_bias):
    nh = rel_bias.shape[0]
    n_keys = PAST_WIN + QUERY_GROUP
    period = n_keys + QUERY_GROUP
    n_far_past = PAST_WIN - MAX_REL + QUERY_GROUP - 1
    n_future = period - n_far_past - rel_bias.shape[1]
    f = jnp.concatenate([
        jnp.broadcast_to(rel_bias[:, -1:], (nh, n_far_past)),
        rel_bias[:, ::-1],
        jnp.broadcast_to(rel_bias[:, :1], (nh, n_future)),
    ], axis=1)
    f = jnp.roll(f, -(QUERY_GROUP - 1), axis=1)
    flat = jnp.tile(f, (1, QUERY_GROUP))[:, :QUERY_GROUP * (period - 1)]
    toeplitz = flat.reshape(nh, QUERY_GROUP, period - 1)[:, :, :n_keys]
    qi = jnp.arange(QUERY_GROUP)[:, None]
    kj = jnp.arange(n_keys)[None, :]
    band_lo = (qi // CHUNK) * CHUNK
    in_band = (kj >= band_lo) & (kj < band_lo + BAND)
    return jnp.where(in_band[None], toeplitz, NEG_INF).astype(F32)


def _mem_attn_kernel(xn_ref, wq_ref, mk_ref, mv_ref, wo_ref, x_ref, gp_ref, gn_ref,
                     xo_ref, xno_ref, acc0_ref, acc1_ref):
    sm_scale = MEM_HEAD_DIM ** -0.5

    def compute():
        q = jnp.dot(xn_ref[...], wq_ref[...], preferred_element_type=F32)
        heads = []
        for h in range(MEM_HEADS):
            c = slice(h * MEM_HEAD_DIM, (h + 1) * MEM_HEAD_DIM)
            head_rows = pl.ds(h, N_MEM, stride=MEM_HEADS)
            s = _qk(q[:, c].astype(BF16), mk_ref[head_rows, :].astype(BF16)) * sm_scale
            p = jnp.exp(s - jnp.max(s, axis=-1, keepdims=True))
            l = jnp.sum(p, axis=-1, keepdims=True)
            o = jnp.dot(p.astype(BF16), mv_ref[head_rows, :].astype(BF16), preferred_element_type=F32) / l
            heads.append(o.astype(BF16))
        return jnp.dot(jnp.concatenate(heads, axis=1), wo_ref[...], preferred_element_type=F32)

    _skewed_step(compute, x_ref, gp_ref, gn_ref, xo_ref, xno_ref, acc0_ref, acc1_ref, 1.0)


def _mem_attn(xn, x, mk, mv, wq, wo, layer, g4, batch, tm):
    t, d = x.shape
    n = t // tm
    tiles_per_seq = n // batch
    cur = lambda i: (jnp.minimum(i, n - 1), 0)
    prev = lambda i: (jnp.maximum(i - 1, 0), 0)
    mem_spec = pl.BlockSpec((None, None, N_MEM * MEM_HEADS, MEM_HEAD_DIM),
                            lambda i: (layer, jnp.minimum(i, n - 1) // tiles_per_seq, 0, 0))
    return pl.pallas_call(
        _mem_attn_kernel,
        out_shape=[jax.ShapeDtypeStruct((t, d), F32), jax.ShapeDtypeStruct((t, d), BF16)],
        grid=(n + 1,),
        in_specs=[
            pl.BlockSpec((tm, d), cur),
            _layer_spec((d, MEM_WIDTH), layer),
            mem_spec, mem_spec,
            _layer_spec((MEM_WIDTH, d), layer),
            pl.BlockSpec((tm, d), prev),
            _gain_spec(layer, 5),
            _gain_spec(layer, 6),
        ],
        out_specs=[pl.BlockSpec((tm, d), prev), pl.BlockSpec((tm, d), prev)],
        scratch_shapes=[pltpu.VMEM((tm, d), F32), pltpu.VMEM((tm, d), F32)],
        compiler_params=_params(("arbitrary",)),
        name="mem_attn",
    )(xn, wq, mk, mv, wo, x, g4, g4)


def _mem_project_kernel(mem_ref, g_ref, wf_ref, k_ref, v_ref, w_ref):
    @pl.when(pl.program_id(1) == 0)
    def _():
        w_ref[...] = wf_ref[...].astype(BF16)

    kv = jnp.dot(_rms(mem_ref[...], g_ref[...]).astype(BF16), w_ref[...], preferred_element_type=F32)
    tm = kv.shape[0]
    for h in range(MEM_HEADS):
        head_rows = pl.ds(h, tm, stride=MEM_HEADS)
        k_ref[head_rows, :] = kv[:, h * MEM_HEAD_DIM:(h + 1) * MEM_HEAD_DIM]
        v_ref[head_rows, :] = kv[:, MEM_WIDTH + h * MEM_HEAD_DIM:MEM_WIDTH + (h + 1) * MEM_HEAD_DIM]


def _mem_project(mem, g_mem, w_kv, tm):
    t, d = mem.shape
    depth = w_kv.shape[0]
    out = jax.ShapeDtypeStruct((depth, t * MEM_HEADS, MEM_HEAD_DIM), F32)
    out_spec = pl.BlockSpec((None, tm * MEM_HEADS, MEM_HEAD_DIM), lambda l, i: (l, i, 0))
    return pl.pallas_call(
        _mem_project_kernel,
        out_shape=[out, out],
        grid=(depth, t // tm),
        in_specs=[
            pl.BlockSpec((tm, d), lambda l, i: (i, 0)),
            pl.BlockSpec((None, 1, d), lambda l, i: (l, 0, 0)),
            pl.BlockSpec((None, d, 2 * MEM_WIDTH), lambda l, i: (l, 0, 0)),
        ],
        out_specs=[out_spec, out_spec],
        scratch_shapes=[pltpu.VMEM((d, 2 * MEM_WIDTH), BF16)],
        compiler_params=_params(("arbitrary", "arbitrary")),
        name="mem_project",
    )(mem, g_mem, w_kv)


def _trunk(x, mem_k, mem_v, weights, *, batch, seq, conv_hist, band_cache, bias_tables):
    prompt = conv_hist is None
    (g4, w_ffn1_in, w_ffn1_out, w_ffn2_in, w_ffn2_out, w_conv_in, w_conv_dw, w_conv_out,
     w_attn_qkv, w_attn_o, w_mem_q, w_mem_o) = weights
    depth = g4.shape[0]
    tm_mem = TM_MEM if prompt else seq

    xn = _rms_cast(x, g4)
    conv_state = band_k = band_v = None
    for i in range(depth):
        m = i // 2
        a = _ffn_in(xn, w_ffn1_in, i)
        x, xn = _mm_res_norm(a, w_ffn1_out, i, x, g4, (i, 1), (i, 2), 0.5, TM_FFN_OUT)
        if i % 2 == 0:
            if prompt:
                z, conv_state = _conv_in_prompt(xn, w_conv_in, w_conv_dw, m, batch, seq)
            else:
                z, conv_state = _conv_in_sample(xn, w_conv_in, w_conv_dw, m, conv_hist, seq)
            x, xn = _mm_res_norm(z, w_conv_out, m, x, g4, (i, 3), (i, 4), 1.0, TM_MIX_OUT)
        else:
            if prompt:
                qkv_hm, band_k, band_v = _qkv(xn, w_attn_qkv, m, seq // TM_QKV)
                o = _band_attn_prompt(qkv_hm, bias_tables[0], batch, seq)
            else:
                qkv_hm, band_k, band_v = _qkv(xn, w_attn_qkv, m, 1)
                o = _band_attn_sample(qkv_hm, band_cache[0], band_cache[1], bias_tables[1], bias_tables[2],
                                      batch, seq)
            x, xn = _mm_res_norm(o, w_attn_o, m, x, g4, (i, 3), (i, 4), 1.0, TM_MIX_OUT)
        x, xn = _mem_attn(xn, x, mem_k, mem_v, w_mem_q, w_mem_o, i, g4, batch, tm_mem)
        a = _ffn_in(xn, w_ffn2_in, i)
        last = i == depth - 1
        x, xn = _mm_res_norm(a, w_ffn2_out, i, x, g4, (i, 7), ((i + 1) % depth, 0), 0.5, TM_FFN_OUT,
                             emit_xn=not last)
    return x, conv_state, band_k, band_v


def kernel(x_prompt, x_sample, state_conv, cache_band_k, cache_band_v, cache_mem_k, cache_mem_v, mem_prompt, g_norm, g_mem, w_ffn1_in, w_ffn1_out, w_ffn2_in, w_ffn2_out, w_conv_in, w_conv_dw, w_conv_out, w_attn_qkv, rel_bias, w_attn_o, w_mem_q, w_mem_kv, w_mem_o):
    batch, seq, d = x_prompt.shape
    dec_batch, dec_seq, _ = x_sample.shape
    depth = g_norm.shape[0]
    assert depth == 2 and state_conv.shape[0] == 1 and cache_band_k.shape[0] == 1
    assert cache_band_k.shape[2] == PAST_WIN and dec_seq == CHUNK and seq % QUERY_GROUP == 0
    assert seq % TM_IN == 0 and seq >= PAST_WIN and TM_IN % dec_seq == 0 and TM_QKV % dec_seq == 0

    weights = (
        g_norm[:, :, None, :],
        w_ffn1_in, w_ffn1_out.astype(BF16), w_ffn2_in, w_ffn2_out.astype(BF16),
        w_conv_in, w_conv_dw, w_conv_out.astype(BF16),
        w_attn_qkv.astype(BF16), w_attn_o.astype(BF16), w_mem_q.astype(BF16), w_mem_o.astype(BF16),
    )
    table = _band_bias_table(rel_bias[0])
    bias_tables = (table, table[:, :CHUNK, :PAST_WIN], table[:, :CHUNK, PAST_WIN:BAND])

    mem_rows = N_MEM * MEM_HEADS
    mem_k_p, mem_v_p = _mem_project(mem_prompt.reshape(batch * N_MEM, d), g_mem[:, None, :], w_mem_kv, 512)

    y_p, conv_p, bk_p, bv_p = _trunk(
        x_prompt.reshape(batch * seq, d),
        mem_k_p.reshape(depth, batch, mem_rows, MEM_HEAD_DIM),
        mem_v_p.reshape(depth, batch, mem_rows, MEM_HEAD_DIM), weights,
        batch=batch, seq=seq, conv_hist=None, band_cache=None, bias_tables=bias_tables)

    y_s, conv_s, bk_s, bv_s = _trunk(
        x_sample.reshape(dec_batch * dec_seq, d),
        cache_mem_k.reshape(depth, dec_batch, mem_rows, MEM_HEAD_DIM),
        cache_mem_v.reshape(depth, dec_batch, mem_rows, MEM_HEAD_DIM), weights,
        batch=dec_batch, seq=dec_seq, conv_hist=state_conv[0],
        band_cache=(cache_band_k.reshape(dec_batch, PAST_WIN * N_HEADS, HEAD_DIM),
                    cache_band_v.reshape(dec_batch, PAST_WIN * N_HEADS, HEAD_DIM)),
        bias_tables=bias_tables)

    mem_shape = (depth, batch, N_MEM, MEM_HEADS, MEM_HEAD_DIM)
    band_p_shape = (1, batch, PAST_WIN, N_HEADS, HEAD_DIM)
    band_s_shape = (1, dec_batch, dec_seq, N_HEADS, HEAD_DIM)
    return (y_p.reshape(batch, seq, d), y_s.reshape(dec_batch, dec_seq, d),
            conv_p[None], bk_p.reshape(band_p_shape), bv_p.reshape(band_p_shape),
            mem_k_p.reshape(mem_shape), mem_v_p.reshape(mem_shape),
            conv_s[None], bk_s.reshape(band_s_shape), bv_s.reshape(band_s_shape))
```

```python
import functools

import jax
import jax.numpy as jnp
from jax import lax
from jax.experimental import pallas as pl
from jax.experimental.pallas import tpu as pltpu

F32 = jnp.float32
BF16 = jnp.bfloat16

D_MODEL = 2048
D_FF = 5632
N_HEADS = 16
HEAD_DIM = 128
CHUNK = 64
PAST_WIN = 512
BAND = PAST_WIN + CHUNK
MAX_REL = 128
N_MEM = 256
MEM_HEADS = 4
MEM_HEAD_DIM = 128
MEM_WIDTH = MEM_HEADS * MEM_HEAD_DIM
EPS = 1e-6
NEG_INF = -1e30
LOG2_E = 1.4426950408889634

V7X_VMEM_LIMIT_BYTES = 58 * 1024 * 1024
BF16_SUBLANES = 16
FF_TILE = 512
CONV_TILE = 512
TM_QKV = PAST_WIN
QUERY_GROUP = 4 * CHUNK
TM_IN = 1024
TM_FFN_IN = 2048
TM_FFN_OUT = 256
TM_MIX_OUT = 512
TM_MEM = 512


def _params(sem):
    return pltpu.CompilerParams(dimension_semantics=sem, vmem_limit_bytes=V7X_VMEM_LIMIT_BYTES)


def _layer_spec(shape, layer):
    zeros = (0,) * len(shape)
    return pl.BlockSpec((None,) + tuple(shape), lambda *_: (layer,) + zeros, pipeline_mode=pl.Buffered(1))


def _gain_spec(layer, n):
    return pl.BlockSpec((None, None, 1, D_MODEL), lambda *_: (layer, n, 0, 0), pipeline_mode=pl.Buffered(1))


def _rms(x, g):
    return x * lax.rsqrt(jnp.mean(x * x, axis=-1, keepdims=True) + EPS) * g


def _rms_cast_kernel(x_ref, g_ref, o_ref):
    o_ref[...] = _rms(x_ref[...], g_ref[...]).astype(BF16)


def _rms_cast(x, g4):
    t, d = x.shape
    return pl.pallas_call(
        _rms_cast_kernel,
        out_shape=jax.ShapeDtypeStruct((t, d), BF16),
        grid=(t // TM_IN,),
        in_specs=[pl.BlockSpec((TM_IN, d), lambda i: (i, 0)), _gain_spec(0, 0)],
        out_specs=pl.BlockSpec((TM_IN, d), lambda i: (i, 0)),
        compiler_params=_params(("arbitrary",)),
        name="rms_cast",
    )(x, g4)


def _res_norm_epilogue(acc_ref, x_ref, gp, gn, xo_ref, xno_ref):
    rows = BF16_SUBLANES
    for r in range(acc_ref.shape[0] // rows):
        sl = slice(r * rows, (r + 1) * rows)
        xnew = x_ref[sl, :] + _rms(acc_ref[sl, :], gp)
        xo_ref[sl, :] = xnew
        if xno_ref is not None:
            xno_ref[sl, :] = _rms(xnew, gn).astype(BF16)


def _skewed_step(compute_tile, x_ref, gp_ref, gn_ref, xo_ref, xno_ref, acc0_ref, acc1_ref, scale):
    i = pl.program_id(0)

    @pl.when(i == 0)
    def _():
        acc1_ref[...] = jnp.zeros_like(acc1_ref)

    def step(prev_ref, cur_ref):
        gp = gp_ref[...] * scale
        gn = gn_ref[...]
        _res_norm_epilogue(prev_ref, x_ref, gp, gn, xo_ref, xno_ref)
        cur_ref[...] = compute_tile()

    @pl.when(i % 2 == 0)
    def _():
        step(acc1_ref, acc0_ref)

    @pl.when(i % 2 == 1)
    def _():
        step(acc0_ref, acc1_ref)


def _load_rows(a_ref):
    if len(a_ref.shape) == 2:
        return a_ref[...]
    return jnp.concatenate([a_ref[h] for h in range(a_ref.shape[0])], axis=1)


def _mm_res_norm_kernel(a_ref, w_ref, x_ref, gp_ref, gn_ref, xo_ref, *rest, scale, emit_xn):
    xno_ref, acc0_ref, acc1_ref = rest if emit_xn else (None,) + rest
    compute = lambda: jnp.dot(_load_rows(a_ref), w_ref[...], preferred_element_type=F32)
    _skewed_step(compute, x_ref, gp_ref, gn_ref, xo_ref, xno_ref, acc0_ref, acc1_ref, scale)


def _mm_res_norm(a, w, w_layer, x, g4, post_gain, next_gain, scale, tm, emit_xn=True):
    t, d = x.shape
    k = w.shape[1]
    n = t // tm
    cur = lambda i: jnp.minimum(i, n - 1)
    prev = lambda i: (jnp.maximum(i - 1, 0), 0)
    if a.ndim == 2:
        a_spec = pl.BlockSpec((tm, k), lambda i: (cur(i), 0))
    else:
        a_spec = pl.BlockSpec((a.shape[0], tm, a.shape[2]), lambda i: (0, cur(i), 0))
    out_shape = [jax.ShapeDtypeStruct((t, d), F32)]
    out_specs = [pl.BlockSpec((tm, d), prev)]
    if emit_xn:
        out_shape.append(jax.ShapeDtypeStruct((t, d), BF16))
        out_specs.append(pl.BlockSpec((tm, d), prev))
    outs = pl.pallas_call(
        functools.partial(_mm_res_norm_kernel, scale=scale, emit_xn=emit_xn),
        out_shape=out_shape,
        grid=(n + 1,),
        in_specs=[
            a_spec,
            _layer_spec((k, d), w_layer),
            pl.BlockSpec((tm, d), prev),
            _gain_spec(*post_gain),
            _gain_spec(*next_gain),
        ],
        out_specs=out_specs,
        scratch_shapes=[pltpu.VMEM((tm, d), F32), pltpu.VMEM((tm, d), F32)],
        compiler_params=_params(("arbitrary",)),
        name="mm_res_norm",
    )(a, w, x, g4, g4)
    return (outs[0], outs[1]) if emit_xn else (outs[0], None)


def _ffn_in_kernel(xn_ref, wg_ref, wu_ref, o_ref, w_ref):
    @pl.when(pl.program_id(1) == 0)
    def _():
        w_ref[:, :FF_TILE] = wg_ref[...].astype(BF16)
        w_ref[:, FF_TILE:] = wu_ref[...].astype(BF16)

    half = xn_ref.shape[0] // 2
    for r in (slice(0, half), slice(half, 2 * half)):
        h = jnp.dot(xn_ref[r, :], w_ref[...], preferred_element_type=F32)
        o_ref[r, :] = (jax.nn.silu(h[:, :FF_TILE]) * h[:, FF_TILE:]).astype(BF16)


def _ffn_in(xn, w_in, layer):
    t, d = xn.shape
    nj = D_FF // FF_TILE
    return pl.pallas_call(
        _ffn_in_kernel,
        out_shape=jax.ShapeDtypeStruct((t, D_FF), BF16),
        grid=(nj, t // TM_FFN_IN),
        in_specs=[
            pl.BlockSpec((TM_FFN_IN, d), lambda j, i: (i, 0)),
            pl.BlockSpec((None, d, FF_TILE), lambda j, i: (layer, 0, j)),
            pl.BlockSpec((None, d, FF_TILE), lambda j, i: (layer, 0, nj + j)),
        ],
        out_specs=pl.BlockSpec((TM_FFN_IN, FF_TILE), lambda j, i: (i, j)),
        scratch_shapes=[pltpu.VMEM((d, 2 * FF_TILE), BF16)],
        compiler_params=_params(("arbitrary", "arbitrary")),
        name="ffn_in",
    )(xn, w_in, w_in)


def _qkv_kernel(xn_ref, w_ref, hm_ref, k_ref, v_ref, *, keep_every):
    j, i = pl.program_id(0), pl.program_id(1)
    acc = jnp.dot(xn_ref[...], w_ref[...], preferred_element_type=F32)
    heads = [acc[:, h * HEAD_DIM:(h + 1) * HEAD_DIM] for h in range(N_HEADS)]
    hm_scale = jnp.where(j == 0, HEAD_DIM ** -0.5 * LOG2_E, 1.0)
    for h in range(N_HEADS):
        hm_ref[h] = (heads[h] * hm_scale).astype(BF16)

    def keep(ref):
        for h in range(N_HEADS):
            ref[pl.ds(h, acc.shape[0], stride=N_HEADS), :] = heads[h]

    kept_tile = i % keep_every == keep_every - 1

    @pl.when((j == 1) & kept_tile)
    def _():
        keep(k_ref)

    @pl.when((j == 2) & kept_tile)
    def _():
        keep(v_ref)


def _qkv(xn, w_qkv, layer, keep_every):
    t, d = xn.shape
    n_keep = t // (TM_QKV * keep_every)
    kept = jax.ShapeDtypeStruct((n_keep, TM_QKV * N_HEADS, HEAD_DIM), F32)

    def k_index(j, i):
        return (jnp.where(j < 1, 0, jnp.where(j > 1, n_keep - 1, i // keep_every)), 0, 0)

    def v_index(j, i):
        return (jnp.where(j < 2, 0, i // keep_every), 0, 0)

    return pl.pallas_call(
        functools.partial(_qkv_kernel, keep_every=keep_every),
        out_shape=[jax.ShapeDtypeStruct((3 * N_HEADS, t, HEAD_DIM), BF16), kept, kept],
        grid=(3, t // TM_QKV),
        in_specs=[
            pl.BlockSpec((TM_QKV, d), lambda j, i: (i, 0)),
            pl.BlockSpec((None, d, D_MODEL), lambda j, i: (layer, 0, j), pipeline_mode=pl.Buffered(1)),
        ],
        out_specs=[
            pl.BlockSpec((N_HEADS, TM_QKV, HEAD_DIM), lambda j, i: (j, i, 0)),
            pl.BlockSpec((None, TM_QKV * N_HEADS, HEAD_DIM), k_index),
            pl.BlockSpec((None, TM_QKV * N_HEADS, HEAD_DIM), v_index),
        ],
        compiler_params=_params(("arbitrary", "arbitrary")),
        name="qkv",
    )(xn, w_qkv)


def _cast_conv_weights(wb_ref, wc_ref, wx_ref, w_ref):
    @pl.when(pl.program_id(1) == 0)
    def _():
        w_ref[:, :CONV_TILE] = wb_ref[...].astype(BF16)
        w_ref[:, CONV_TILE:2 * CONV_TILE] = wc_ref[...].astype(BF16)
        w_ref[:, 2 * CONV_TILE:] = wx_ref[...].astype(BF16)


def _gated_conv(h, hist0, hist1, wdw, seq):
    tm = h.shape[0]
    nseq = tm // seq
    bg = h[:, :CONV_TILE]
    u = h[:, CONV_TILE:2 * CONV_TILE] * h[:, 2 * CONV_TILE:]
    u3 = u.reshape(nseq, seq, CONV_TILE)
    row = lax.broadcasted_iota(jnp.int32, u3.shape, 1)
    prev1 = pltpu.roll(u, 1, axis=0).reshape(u3.shape)
    prev2 = pltpu.roll(u, 2, axis=0).reshape(u3.shape)
    prev1 = jnp.where(row == 0, hist1, prev1)
    prev2 = jnp.where(row == 0, hist0, jnp.where(row == 1, hist1, prev2))
    y = wdw[0:1, :] * prev2 + wdw[1:2, :] * prev1 + wdw[2:3, :] * u3
    z = (bg.reshape(u3.shape) * y).astype(BF16).reshape(tm, CONV_TILE)
    return z, u3


def _conv_in_prompt_kernel(xn_ref, wb_ref, wc_ref, wx_ref, wdw_ref, z_ref, st_ref, w_ref, carry_ref,
                           *, tiles_per_seq):
    tm = xn_ref.shape[0]
    _cast_conv_weights(wb_ref, wc_ref, wx_ref, w_ref)

    @pl.when(pl.program_id(1) % tiles_per_seq == 0)
    def _():
        carry_ref[...] = jnp.zeros_like(carry_ref)

    h = jnp.dot(xn_ref[...], w_ref[...], preferred_element_type=F32)
    hist0 = carry_ref[0:1, :].reshape(1, 1, CONV_TILE)
    hist1 = carry_ref[1:2, :].reshape(1, 1, CONV_TILE)
    z, u3 = _gated_conv(h, hist0, hist1, wdw_ref[...], tm)
    z_ref[...] = z
    last = u3[0, tm - 2:, :]
    carry_ref[0:2, :] = last
    st_ref[0] = last


def _conv_weight_specs(d, layer):
    nj = d // CONV_TILE
    part = lambda p: pl.BlockSpec((None, d, CONV_TILE), lambda j, i: (layer, 0, p * nj + j),
                                  pipeline_mode=pl.Buffered(1))
    return [part(0), part(1), part(2), pl.BlockSpec((None, 3, CONV_TILE), lambda j, i: (layer, 0, j))]


def _conv_in_prompt(xn, w_in, w_dw, layer, batch, seq):
    t, d = xn.shape
    tiles_per_seq = seq // TM_IN
    return pl.pallas_call(
        functools.partial(_conv_in_prompt_kernel, tiles_per_seq=tiles_per_seq),
        out_shape=[jax.ShapeDtypeStruct((t, d), BF16), jax.ShapeDtypeStruct((batch, 2, d), F32)],
        grid=(d // CONV_TILE, t // TM_IN),
        in_specs=[pl.BlockSpec((TM_IN, d), lambda j, i: (i, 0))] + _conv_weight_specs(d, layer),
        out_specs=[
            pl.BlockSpec((TM_IN, CONV_TILE), lambda j, i: (i, j)),
            pl.BlockSpec((1, 2, CONV_TILE), lambda j, i: (i // tiles_per_seq, 0, j)),
        ],
        scratch_shapes=[pltpu.VMEM((d, 3 * CONV_TILE), BF16), pltpu.VMEM((8, CONV_TILE), F32)],
        compiler_params=_params(("arbitrary", "arbitrary")),
        name="conv_in_prompt",
    )(xn, w_in, w_in, w_in, w_dw)


def _conv_in_sample_kernel(xn_ref, wb_ref, wc_ref, wx_ref, wdw_ref, hist_ref, z_ref, st_ref, w_ref, *, seq):
    _cast_conv_weights(wb_ref, wc_ref, wx_ref, w_ref)
    h = jnp.dot(xn_ref[...], w_ref[...], preferred_element_type=F32)
    hist = hist_ref[...]
    z, u3 = _gated_conv(h, hist[:, 0:1, :], hist[:, 1:2, :], wdw_ref[...], seq)
    z_ref[...] = z
    st_ref[...] = u3[:, seq - 2:, :]


def _conv_in_sample(xn, w_in, w_dw, layer, hist, seq):
    t, d = xn.shape
    nseq = TM_IN // seq
    hist_spec = pl.BlockSpec((nseq, 2, CONV_TILE), lambda j, i: (i, 0, j))
    return pl.pallas_call(
        functools.partial(_conv_in_sample_kernel, seq=seq),
        out_shape=[jax.ShapeDtypeStruct((t, d), BF16), jax.ShapeDtypeStruct(hist.shape, F32)],
        grid=(d // CONV_TILE, t // TM_IN),
        in_specs=[pl.BlockSpec((TM_IN, d), lambda j, i: (i, 0))] + _conv_weight_specs(d, layer) + [hist_spec],
        out_specs=[pl.BlockSpec((TM_IN, CONV_TILE), lambda j, i: (i, j)), hist_spec],
        scratch_shapes=[pltpu.VMEM((d, 3 * CONV_TILE), BF16)],
        compiler_params=_params(("arbitrary", "arbitrary")),
        name="conv_in_sample",
    )(xn, w_in, w_in, w_in, w_dw, hist)


def _qk(q, k):
    return lax.dot_general(q, k, (((1,), (1,)), ((), ())), preferred_element_type=F32)


def _band_attn_prompt_kernel(q_ref, k_ref, v_ref, bias_ref, o_ref):
    seq = q_ref.shape[0]
    n_groups = seq // QUERY_GROUP

    def key_range(m):
        q0 = m * QUERY_GROUP
        return max(0, q0 - PAST_WIN), q0 + QUERY_GROUP

    def scores(m):
        q0 = m * QUERY_GROUP
        lo, hi = key_range(m)
        col0 = lo - (q0 - PAST_WIN)
        return _qk(q_ref[q0:hi, :], k_ref[lo:hi, :]) + bias_ref[:, col0:col0 + hi - lo]

    s_next = scores(0)
    for m in range(n_groups):
        s = s_next
        if m + 1 < n_groups:
            s_next = scores(m + 1)
        lo, hi = key_range(m)
        p = jnp.exp2(s - jnp.max(s, axis=-1, keepdims=True))
        l = jnp.sum(p, axis=-1, keepdims=True)
        o = jnp.dot(p.astype(BF16), v_ref[lo:hi, :], preferred_element_type=F32) / l
        o_ref[m * QUERY_GROUP:hi, :] = o.astype(BF16)


def _band_attn_prompt(qkv_hm, bias, batch, seq):
    head = lambda off: pl.BlockSpec((None, seq, HEAD_DIM), lambda h, b: (off + h, b, 0))
    return pl.pallas_call(
        _band_attn_prompt_kernel,
        out_shape=jax.ShapeDtypeStruct((N_HEADS, batch * seq, HEAD_DIM), BF16),
        grid=(N_HEADS, batch),
        in_specs=[head(0), head(N_HEADS), head(2 * N_HEADS),
                  pl.BlockSpec((None,) + bias.shape[1:], lambda h, b: (h, 0, 0))],
        out_specs=head(0),
        compiler_params=_params(("arbitrary", "arbitrary")),
        name="band_attn_prompt",
    )(qkv_hm, qkv_hm, qkv_hm, bias)


def _band_attn_sample_kernel(qkv_ref, ck_ref, cv_ref, bias_c_ref, bias_n_ref, o_ref):
    for h in range(N_HEADS):
        q, kn, vn = qkv_ref[h], qkv_ref[N_HEADS + h], qkv_ref[2 * N_HEADS + h]
        head_rows = pl.ds(h, PAST_WIN, stride=N_HEADS)
        kc = ck_ref[head_rows, :].astype(BF16)
        vc = cv_ref[head_rows, :].astype(BF16)
        s_c = _qk(q, kc) + bias_c_ref[h]
        s_n = _qk(q, kn) + bias_n_ref[h]
        mx = jnp.maximum(jnp.max(s_c, axis=-1, keepdims=True), jnp.max(s_n, axis=-1, keepdims=True))
        p_c = jnp.exp2(s_c - mx)
        p_n = jnp.exp2(s_n - mx)
        l = jnp.sum(p_c, axis=-1, keepdims=True) + jnp.sum(p_n, axis=-1, keepdims=True)
        o = (jnp.dot(p_c.astype(BF16), vc, preferred_element_type=F32)
             + jnp.dot(p_n.astype(BF16), vn, preferred_element_type=F32)) / l
        o_ref[h] = o.astype(BF16)


def _band_attn_sample(qkv_hm, ck, cv, bias_c, bias_n, batch, seq):
    cache = pl.BlockSpec((None, PAST_WIN * N_HEADS, HEAD_DIM), lambda b: (b, 0, 0))
    const = lambda a: pl.BlockSpec(a.shape, lambda b: (0,) * a.ndim, pipeline_mode=pl.Buffered(1))
    return pl.pallas_call(
        _band_attn_sample_kernel,
        out_shape=jax.ShapeDtypeStruct((N_HEADS, batch * seq, HEAD_DIM), BF16),
        grid=(batch,),
        in_specs=[
            pl.BlockSpec((3 * N_HEADS, seq, HEAD_DIM), lambda b: (0, b, 0)),
            cache, cache,
            const(bias_c), const(bias_n),
        ],
        out_specs=pl.BlockSpec((N_HEADS, seq, HEAD_DIM), lambda b: (0, b, 0)),
        compiler_params=_params(("arbitrary",)),
        name="band_attn_sample",
    )(qkv_hm, ck, cv, bias_c, bias_n)


def _band_bias_table(rel_bias):
    nh = rel_bias.shape[0]
    n_keys = PAST_WIN + QUERY_GROUP
    period = n_keys + QUERY_GROUP
    n_far_past = PAST_WIN - MAX_REL + QUERY_GROUP - 1
    n_future = period - n_far_past - rel_bias.shape[1]
    f = jnp.concatenate([
        jnp.broadcast_to(rel_bias[:, -1:], (nh, n_far_past)),
        rel_bias[:, ::-1],
        jnp.broadcast_to(rel_bias[:, :1], (nh, n_future)),
    ], axis=1)
    f = jnp.roll(f, -(QUERY_GROUP - 1), axis=1)
    flat = jnp.tile(f, (1, QUERY_GROUP))[:, :QUERY_GROUP * (period - 1)]
    toeplitz = flat.reshape(nh, QUERY_GROUP, period - 1)[:, :, :n_keys]
    qi = jnp.arange(QUERY_GROUP)[:, None]
    kj = jnp.arange(n_keys)[None, :]
    band_lo = (qi // CHUNK) * CHUNK
    in_band = (kj >= band_lo) & (kj < band_lo + BAND)
    return jnp.where(in_band[None], toeplitz * LOG2_E, NEG_INF).astype(F32)


def _mem_attn_kernel(xn_ref, wq_ref, mk_ref, mv_ref, wo_ref, x_ref, gp_ref, gn_ref,
                     xo_ref, xno_ref, acc0_ref, acc1_ref):
    def compute():
        q = jnp.dot(xn_ref[...], wq_ref[...], preferred_element_type=F32) * (MEM_HEAD_DIM ** -0.5 * LOG2_E)
        heads = []
        for h in range(MEM_HEADS):
            c = slice(h * MEM_HEAD_DIM, (h + 1) * MEM_HEAD_DIM)
            head_rows = pl.ds(h, N_MEM, stride=MEM_HEADS)
            s = _qk(q[:, c].astype(BF16), mk_ref[head_rows, :].astype(BF16))
            p = jnp.exp2(s - jnp.max(s, axis=-1, keepdims=True))
            l = jnp.sum(p, axis=-1, keepdims=True)
            o = jnp.dot(p.astype(BF16), mv_ref[head_rows, :].astype(BF16), preferred_element_type=F32) / l
            heads.append(o.astype(BF16))
        return jnp.dot(jnp.concatenate(heads, axis=1), wo_ref[...], preferred_element_type=F32)

    _skewed_step(compute, x_ref, gp_ref, gn_ref, xo_ref, xno_ref, acc0_ref, acc1_ref, 1.0)


def _mem_attn(xn, x, mk, mv, wq, wo, layer, g4, batch, tm):
    t, d = x.shape
    n = t // tm
    tiles_per_seq = n // batch
    cur = lambda i: (jnp.minimum(i, n - 1), 0)
    prev = lambda i: (jnp.maximum(i - 1, 0), 0)
    mem_spec = pl.BlockSpec((None, None, N_MEM * MEM_HEADS, MEM_HEAD_DIM),
                            lambda i: (layer, jnp.minimum(i, n - 1) // tiles_per_seq, 0, 0))
    return pl.pallas_call(
        _mem_attn_kernel,
        out_shape=[jax.ShapeDtypeStruct((t, d), F32), jax.ShapeDtypeStruct((t, d), BF16)],
        grid=(n + 1,),
        in_specs=[
            pl.BlockSpec((tm, d), cur),
            _layer_spec((d, MEM_WIDTH), layer),
            mem_spec, mem_spec,
            _layer_spec((MEM_WIDTH, d), layer),
            pl.BlockSpec((tm, d), prev),
            _gain_spec(layer, 5),
            _gain_spec(layer, 6),
        ],
        out_specs=[pl.BlockSpec((tm, d), prev), pl.BlockSpec((tm, d), prev)],
        scratch_shapes=[pltpu.VMEM((tm, d), F32), pltpu.VMEM((tm, d), F32)],
        compiler_params=_params(("arbitrary",)),
        name="mem_attn",
    )(xn, wq, mk, mv, wo, x, g4, g4)


def _mem_project_kernel(mem_ref, g_ref, wf_ref, k_ref, v_ref, w_ref):
    @pl.when(pl.program_id(1) == 0)
    def _():
        w_ref[...] = wf_ref[...].astype(BF16)

    kv = jnp.dot(_rms(mem_ref[...], g_ref[...]).astype(BF16), w_ref[...], preferred_element_type=F32)
    tm = kv.shape[0]
    for h in range(MEM_HEADS):
        head_rows = pl.ds(h, tm, stride=MEM_HEADS)
        k_ref[head_rows, :] = kv[:, h * MEM_HEAD_DIM:(h + 1) * MEM_HEAD_DIM]
        v_ref[head_rows, :] = kv[:, MEM_WIDTH + h * MEM_HEAD_DIM:MEM_WIDTH + (h + 1) * MEM_HEAD_DIM]


def _mem_project(mem, g_mem, w_kv, tm):
    t, d = mem.shape
    depth = w_kv.shape[0]
    out = jax.ShapeDtypeStruct((depth, t * MEM_HEADS, MEM_HEAD_DIM), F32)
    out_spec = pl.BlockSpec((None, tm * MEM_HEADS, MEM_HEAD_DIM), lambda l, i: (l, i, 0))
    return pl.pallas_call(
        _mem_project_kernel,
        out_shape=[out, out],
        grid=(depth, t // tm),
        in_specs=[
            pl.BlockSpec((tm, d), lambda l, i: (i, 0)),
            pl.BlockSpec((None, 1, d), lambda l, i: (l, 0, 0)),
            pl.BlockSpec((None, d, 2 * MEM_WIDTH), lambda l, i: (l, 0, 0)),
        ],
        out_specs=[out_spec, out_spec],
        scratch_shapes=[pltpu.VMEM((d, 2 * MEM_WIDTH), BF16)],
        compiler_params=_params(("arbitrary", "arbitrary")),
        name="mem_project",
    )(mem, g_mem, w_kv)


def _trunk(x, mem_k, mem_v, weights, *, batch, seq, conv_hist, band_cache, bias_tables):
    prompt = conv_hist is None
    (g4, w_ffn1_in, w_ffn1_out, w_ffn2_in, w_ffn2_out, w_conv_in, w_conv_dw, w_conv_out,
     w_attn_qkv, w_attn_o, w_mem_q, w_mem_o) = weights
    depth = g4.shape[0]
    tm_mem = TM_MEM if prompt else seq

    xn = _rms_cast(x, g4)
    conv_state = band_k = band_v = None
    for i in range(depth):
        m = i // 2
        a = _ffn_in(xn, w_ffn1_in, i)
        x, xn = _mm_res_norm(a, w_ffn1_out, i, x, g4, (i, 1), (i, 2), 0.5, TM_FFN_OUT)
        if i % 2 == 0:
            if prompt:
                z, conv_state = _conv_in_prompt(xn, w_conv_in, w_conv_dw, m, batch, seq)
            else:
                z, conv_state = _conv_in_sample(xn, w_conv_in, w_conv_dw, m, conv_hist, seq)
            x, xn = _mm_res_norm(z, w_conv_out, m, x, g4, (i, 3), (i, 4), 1.0, TM_MIX_OUT)
        else:
            if prompt:
                qkv_hm, band_k, band_v = _qkv(xn, w_attn_qkv, m, seq // TM_QKV)
                o = _band_attn_prompt(qkv_hm, bias_tables[0], batch, seq)
            else:
                qkv_hm, band_k, band_v = _qkv(xn, w_attn_qkv, m, 1)
                o = _band_attn_sample(qkv_hm, band_cache[0], band_cache[1], bias_tables[1], bias_tables[2],
                                      batch, seq)
            x, xn = _mm_res_norm(o, w_attn_o, m, x, g4, (i, 3), (i, 4), 1.0, TM_MIX_OUT)
        x, xn = _mem_attn(xn, x, mem_k, mem_v, w_mem_q, w_mem_o, i, g4, batch, tm_mem)
        a = _ffn_in(xn, w_ffn2_in, i)
        last = i == depth - 1
        x, xn = _mm_res_norm(a, w_ffn2_out, i, x, g4, (i, 7), ((i + 1) % depth, 0), 0.5, TM_FFN_OUT,
                             emit_xn=not last)
    return x, conv_state, band_k, band_v


def kernel(x_prompt, x_sample, state_conv, cache_band_k, cache_band_v, cache_mem_k, cache_mem_v, mem_prompt, g_norm, g_mem, w_ffn1_in, w_ffn1_out, w_ffn2_in, w_ffn2_out, w_conv_in, w_conv_dw, w_conv_out, w_attn_qkv, rel_bias, w_attn_o, w_mem_q, w_mem_kv, w_mem_o):
    batch, seq, d = x_prompt.shape
    dec_batch, dec_seq, _ = x_sample.shape
    depth = g_norm.shape[0]
    assert depth == 2 and state_conv.shape[0] == 1 and cache_band_k.shape[0] == 1
    assert cache_band_k.shape[2] == PAST_WIN and dec_seq == CHUNK and seq % QUERY_GROUP == 0
    assert seq % TM_IN == 0 and seq >= PAST_WIN and TM_IN % dec_seq == 0 and TM_QKV % dec_seq == 0

    weights = (
        g_norm[:, :, None, :],
        w_ffn1_in, w_ffn1_out.astype(BF16), w_ffn2_in, w_ffn2_out.astype(BF16),
        w_conv_in, w_conv_dw, w_conv_out.astype(BF16),
        w_attn_qkv.astype(BF16), w_attn_o.astype(BF16), w_mem_q.astype(BF16), w_mem_o.astype(BF16),
    )
    table = _band_bias_table(rel_bias[0])
    bias_tables = (table, table[:, :CHUNK, :PAST_WIN], table[:, :CHUNK, PAST_WIN:BAND])

    mem_rows = N_MEM * MEM_HEADS
    mem_k_p, mem_v_p = _mem_project(mem_prompt.reshape(batch * N_MEM, d), g_mem[:, None, :], w_mem_kv, 512)

    y_p, conv_p, bk_p, bv_p = _trunk(
        x_prompt.reshape(batch * seq, d),
        mem_k_p.reshape(depth, batch, mem_rows, MEM_HEAD_DIM),
        mem_v_p.reshape(depth, batch, mem_rows, MEM_HEAD_DIM), weights,
        batch=batch, seq=seq, conv_hist=None, band_cache=None, bias_tables=bias_tables)

    y_s, conv_s, bk_s, bv_s = _trunk(
        x_sample.reshape(dec_batch * dec_seq, d),
        cache_mem_k.reshape(depth, dec_batch, mem_rows, MEM_HEAD_DIM),
        cache_mem_v.reshape(depth, dec_batch, mem_rows, MEM_HEAD_DIM), weights,
        batch=dec_batch, seq=dec_seq, conv_hist=state_conv[0],
        band_cache=(cache_band_k.reshape(dec_batch, PAST_WIN * N_HEADS, HEAD_DIM),
                    cache_band_v.reshape(dec_batch, PAST_WIN * N_HEADS, HEAD_DIM)),
        bias_tables=bias_tables)

    mem_shape = (depth, batch, N_MEM, MEM_HEADS, MEM_HEAD_DIM)
    band_p_shape = (1, batch, PAST_WIN, N_HEADS, HEAD_DIM)
    band_s_shape = (1, dec_batch, dec_seq, N_HEADS, HEAD_DIM)
    return (y_p.reshape(batch, seq, d), y_s.reshape(dec_batch, dec_seq, d),
            conv_p[None], bk_p.reshape(band_p_shape), bv_p.reshape(band_p_shape),
            mem_k_p.reshape(mem_shape), mem_v_p.reshape(mem_shape),
            conv_s[None], bk_s.reshape(band_s_shape), bv_s.reshape(band_s_shape))
```

```python
import functools

import jax
import jax.numpy as jnp
from jax import lax
from jax.experimental import pallas as pl
from jax.experimental.pallas import tpu as pltpu

F32 = jnp.float32
BF16 = jnp.bfloat16

D_MODEL = 2048
D_FF = 5632
N_HEADS = 16
HEAD_DIM = 128
CHUNK = 64
PAST_WIN = 512
BAND = PAST_WIN + CHUNK
MAX_REL = 128
N_MEM = 256
MEM_HEADS = 4
MEM_HEAD_DIM = 128
MEM_WIDTH = MEM_HEADS * MEM_HEAD_DIM
EPS = 1e-6
NEG_INF = -1e30
LOG2_E = 1.4426950408889634

V7X_VMEM_LIMIT_BYTES = 58 * 1024 * 1024
BF16_SUBLANES = 16
LANES = 128
EPILOGUE_PIECES = 4
FF_TILE = 512
CONV_TILE = 512
TM_QKV = PAST_WIN
QUERY_GROUP = 4 * CHUNK
TM_IN = 1024
TM_FFN_IN = 2048
TM_FFN_OUT = 256
TM_MIX_OUT = 512
TM_MEM = 512
TM_MEM_SAMPLE = 256


def _params(sem):
    return pltpu.CompilerParams(dimension_semantics=sem, vmem_limit_bytes=V7X_VMEM_LIMIT_BYTES)


def _layer_spec(shape, layer):
    zeros = (0,) * len(shape)
    return pl.BlockSpec((None,) + tuple(shape), lambda *_: (layer,) + zeros, pipeline_mode=pl.Buffered(1))


def _gain_spec(layer, n):
    return pl.BlockSpec((None, None, 1, D_MODEL), lambda *_: (layer, n, 0, 0), pipeline_mode=pl.Buffered(1))


def _rms(x, g):
    return x * lax.rsqrt(jnp.mean(x * x, axis=-1, keepdims=True) + EPS) * g


def _rms_cast_kernel(x_ref, g_ref, o_ref):
    o_ref[...] = _rms(x_ref[...], g_ref[...]).astype(BF16)


def _rms_cast(x, g4):
    t, d = x.shape
    return pl.pallas_call(
        _rms_cast_kernel,
        out_shape=jax.ShapeDtypeStruct((t, d), BF16),
        grid=(t // TM_IN,),
        in_specs=[pl.BlockSpec((TM_IN, d), lambda i: (i, 0)), _gain_spec(0, 0)],
        out_specs=pl.BlockSpec((TM_IN, d), lambda i: (i, 0)),
        compiler_params=_params(("arbitrary",)),
        name="rms_cast",
    )(x, g4)


def _res_norm_epilogue(acc_ref, x_ref, gp, gn, xo_ref, xno_ref, row0, n_rows):
    rows = BF16_SUBLANES
    xnew = None
    for r in range(row0, row0 + n_rows, rows):
        sl = slice(r, r + rows)
        xnew = x_ref[sl, :] + _rms(acc_ref[sl, :], gp)
        xo_ref[sl, :] = xnew
        if xno_ref is not None:
            xno_ref[sl, :] = _rms(xnew, gn).astype(BF16)
    return xnew[:, :LANES]


def _zero_from(x):
    sixteen = jnp.uint32(16)
    bits = lax.bitcast_convert_type(x, jnp.uint32)
    return lax.shift_right_logical(lax.shift_right_logical(bits, sixteen), sixteen).astype(F32)


def _after(lhs, zero):
    r = BF16_SUBLANES
    first = (lhs[:r, :LANES].astype(F32) + zero).astype(lhs.dtype)
    top = jnp.concatenate([first, lhs[:r, LANES:]], axis=1)
    return jnp.concatenate([top, lhs[r:, :]], axis=0)


def _skewed_step(load_lhs, w_ref, x_ref, gp_ref, gn_ref, xo_ref, xno_ref, acc0_ref, acc1_ref, scale):
    i = pl.program_id(0)
    tm, d = acc0_ref.shape
    rows = tm // EPILOGUE_PIECES
    cols = d // EPILOGUE_PIECES

    @pl.when(i == 0)
    def _():
        acc1_ref[...] = jnp.zeros_like(acc1_ref)

    def step(prev_ref, cur_ref):
        gp = gp_ref[...] * scale
        gn = gn_ref[...]
        lhs = load_lhs()
        for k in range(EPILOGUE_PIECES):
            c = slice(k * cols, (k + 1) * cols)
            cur_ref[:, c] = jnp.dot(lhs, w_ref[:, c], preferred_element_type=F32)
            done = _res_norm_epilogue(prev_ref, x_ref, gp, gn, xo_ref, xno_ref, k * rows, rows)
            lhs = _after(lhs, _zero_from(done))

    @pl.when(i % 2 == 0)
    def _():
        step(acc1_ref, acc0_ref)

    @pl.when(i % 2 == 1)
    def _():
        step(acc0_ref, acc1_ref)


def _load_rows(a_ref):
    if len(a_ref.shape) == 2:
        return a_ref[...]
    return jnp.concatenate([a_ref[h] for h in range(a_ref.shape[0])], axis=1)


def _mm_res_norm_kernel(a_ref, w_ref, x_ref, gp_ref, gn_ref, xo_ref, *rest, scale, emit_xn):
    xno_ref, acc0_ref, acc1_ref = rest if emit_xn else (None,) + rest
    _skewed_step(lambda: _load_rows(a_ref), w_ref, x_ref, gp_ref, gn_ref, xo_ref, xno_ref,
                 acc0_ref, acc1_ref, scale)


def _mm_res_norm(a, w, w_layer, x, g4, post_gain, next_gain, scale, tm, emit_xn=True):
    t, d = x.shape
    k = w.shape[1]
    n = t // tm
    cur = lambda i: jnp.minimum(i, n - 1)
    prev = lambda i: (jnp.maximum(i - 1, 0), 0)
    if a.ndim == 2:
        a_spec = pl.BlockSpec((tm, k), lambda i: (cur(i), 0))
    else:
        a_spec = pl.BlockSpec((a.shape[0], tm, a.shape[2]), lambda i: (0, cur(i), 0))
    out_shape = [jax.ShapeDtypeStruct((t, d), F32)]
    out_specs = [pl.BlockSpec((tm, d), prev)]
    if emit_xn:
        out_shape.append(jax.ShapeDtypeStruct((t, d), BF16))
        out_specs.append(pl.BlockSpec((tm, d), prev))
    outs = pl.pallas_call(
        functools.partial(_mm_res_norm_kernel, scale=scale, emit_xn=emit_xn),
        out_shape=out_shape,
        grid=(n + 1,),
        in_specs=[
            a_spec,
            _layer_spec((k, d), w_layer),
            pl.BlockSpec((tm, d), prev),
            _gain_spec(*post_gain),
            _gain_spec(*next_gain),
        ],
        out_specs=out_specs,
        scratch_shapes=[pltpu.VMEM((tm, d), F32), pltpu.VMEM((tm, d), F32)],
        compiler_params=_params(("arbitrary",)),
        name="mm_res_norm",
    )(a, w, x, g4, g4)
    return (outs[0], outs[1]) if emit_xn else (outs[0], None)


def _ffn_in_kernel(xn_ref, wg_ref, wu_ref, *rest, cast_w_out):
    if cast_w_out:
        wo_f32_ref, o_ref, wo_bf16_ref, w_ref = rest
        wo_bf16_ref[...] = wo_f32_ref[...].astype(BF16)
    else:
        o_ref, w_ref = rest

    @pl.when(pl.program_id(1) == 0)
    def _():
        w_ref[:, :FF_TILE] = wg_ref[...].astype(BF16)
        w_ref[:, FF_TILE:] = wu_ref[...].astype(BF16)

    half = xn_ref.shape[0] // 2
    for r in (slice(0, half), slice(half, 2 * half)):
        h = jnp.dot(xn_ref[r, :], w_ref[...], preferred_element_type=F32)
        o_ref[r, :] = (jax.nn.silu(h[:, :FF_TILE]) * h[:, FF_TILE:]).astype(BF16)


def _ffn_in(xn, w_in, layer, w_out=None):
    t, d = xn.shape
    nj = D_FF // FF_TILE
    ni = t // TM_FFN_IN
    in_specs = [
        pl.BlockSpec((TM_FFN_IN, d), lambda j, i: (i, 0)),
        pl.BlockSpec((None, d, FF_TILE), lambda j, i: (layer, 0, j)),
        pl.BlockSpec((None, d, FF_TILE), lambda j, i: (layer, 0, nj + j)),
    ]
    out_shape = [jax.ShapeDtypeStruct((t, D_FF), BF16)]
    out_specs = [pl.BlockSpec((TM_FFN_IN, FF_TILE), lambda j, i: (i, j))]
    operands = [xn, w_in, w_in]
    if w_out is not None:
        rows = D_FF // (nj * ni)
        assert rows * nj * ni == D_FF and rows % BF16_SUBLANES == 0
        in_specs.append(pl.BlockSpec((None, rows, d), lambda j, i: (layer, j * ni + i, 0)))
        out_shape.append(jax.ShapeDtypeStruct((1, D_FF, d), BF16))
        out_specs.append(pl.BlockSpec((None, rows, d), lambda j, i: (0, j * ni + i, 0)))
        operands.append(w_out)
    outs = pl.pallas_call(
        functools.partial(_ffn_in_kernel, cast_w_out=w_out is not None),
        out_shape=out_shape,
        grid=(nj, ni),
        in_specs=in_specs,
        out_specs=out_specs,
        scratch_shapes=[pltpu.VMEM((d, 2 * FF_TILE), BF16)],
        compiler_params=_params(("arbitrary", "arbitrary")),
        name="ffn_in",
    )(*operands)
    return outs if w_out is not None else outs[0]


def _qkv_kernel(xn_ref, w_ref, hm_ref, k_ref, v_ref, *, keep_every):
    j, i = pl.program_id(0), pl.program_id(1)
    acc = jnp.dot(xn_ref[...], w_ref[...], preferred_element_type=F32)
    heads = [acc[:, h * HEAD_DIM:(h + 1) * HEAD_DIM] for h in range(N_HEADS)]
    hm_scale = jnp.where(j == 0, HEAD_DIM ** -0.5 * LOG2_E, 1.0)
    for h in range(N_HEADS):
        hm_ref[h] = (heads[h] * hm_scale).astype(BF16)

    def keep(ref):
        for h in range(N_HEADS):
            ref[pl.ds(h, acc.shape[0], stride=N_HEADS), :] = heads[h]

    kept_tile = i % keep_every == keep_every - 1

    @pl.when((j == 1) & kept_tile)
    def _():
        keep(k_ref)

    @pl.when((j == 2) & kept_tile)
    def _():
        keep(v_ref)


def _qkv(xn, w_qkv, layer, keep_every):
    t, d = xn.shape
    n_keep = t // (TM_QKV * keep_every)
    kept = jax.ShapeDtypeStruct((n_keep, TM_QKV * N_HEADS, HEAD_DIM), F32)

    def k_index(j, i):
        return (jnp.where(j < 1, 0, jnp.where(j > 1, n_keep - 1, i // keep_every)), 0, 0)

    def v_index(j, i):
        return (jnp.where(j < 2, 0, i // keep_every), 0, 0)

    return pl.pallas_call(
        functools.partial(_qkv_kernel, keep_every=keep_every),
        out_shape=[jax.ShapeDtypeStruct((3 * N_HEADS, t, HEAD_DIM), BF16), kept, kept],
        grid=(3, t // TM_QKV),
        in_specs=[
            pl.BlockSpec((TM_QKV, d), lambda j, i: (i, 0)),
            pl.BlockSpec((None, d, D_MODEL), lambda j, i: (layer, 0, j), pipeline_mode=pl.Buffered(1)),
        ],
        out_specs=[
            pl.BlockSpec((N_HEADS, TM_QKV, HEAD_DIM), lambda j, i: (j, i, 0)),
            pl.BlockSpec((None, TM_QKV * N_HEADS, HEAD_DIM), k_index),
            pl.BlockSpec((None, TM_QKV * N_HEADS, HEAD_DIM), v_index),
        ],
        compiler_params=_params(("arbitrary", "arbitrary")),
        name="qkv",
    )(xn, w_qkv)


def _cast_conv_weights(wb_ref, wc_ref, wx_ref, w_ref):
    @pl.when(pl.program_id(1) == 0)
    def _():
        w_ref[:, :CONV_TILE] = wb_ref[...].astype(BF16)
        w_ref[:, CONV_TILE:2 * CONV_TILE] = wc_ref[...].astype(BF16)
        w_ref[:, 2 * CONV_TILE:] = wx_ref[...].astype(BF16)


def _gated_conv(h, hist0, hist1, wdw, seq):
    tm = h.shape[0]
    nseq = tm // seq
    bg = h[:, :CONV_TILE]
    u = h[:, CONV_TILE:2 * CONV_TILE] * h[:, 2 * CONV_TILE:]
    u3 = u.reshape(nseq, seq, CONV_TILE)
    row = lax.broadcasted_iota(jnp.int32, u3.shape, 1)
    prev1 = pltpu.roll(u, 1, axis=0).reshape(u3.shape)
    prev2 = pltpu.roll(u, 2, axis=0).reshape(u3.shape)
    prev1 = jnp.where(row == 0, hist1, prev1)
    prev2 = jnp.where(row == 0, hist0, jnp.where(row == 1, hist1, prev2))
    y = wdw[0:1, :] * prev2 + wdw[1:2, :] * prev1 + wdw[2:3, :] * u3
    z = (bg.reshape(u3.shape) * y).astype(BF16).reshape(tm, CONV_TILE)
    return z, u3


def _conv_in_prompt_kernel(xn_ref, wb_ref, wc_ref, wx_ref, wdw_ref, z_ref, st_ref, w_ref, carry_ref,
                           *, tiles_per_seq):
    tm = xn_ref.shape[0]
    _cast_conv_weights(wb_ref, wc_ref, wx_ref, w_ref)

    @pl.when(pl.program_id(1) % tiles_per_seq == 0)
    def _():
        carry_ref[...] = jnp.zeros_like(carry_ref)

    h = jnp.dot(xn_ref[...], w_ref[...], preferred_element_type=F32)
    hist0 = carry_ref[0:1, :].reshape(1, 1, CONV_TILE)
    hist1 = carry_ref[1:2, :].reshape(1, 1, CONV_TILE)
    z, u3 = _gated_conv(h, hist0, hist1, wdw_ref[...], tm)
    z_ref[...] = z
    last = u3[0, tm - 2:, :]
    carry_ref[0:2, :] = last
    st_ref[0] = last


def _conv_weight_specs(d, layer):
    nj = d // CONV_TILE
    part = lambda p: pl.BlockSpec((None, d, CONV_TILE), lambda j, i: (layer, 0, p * nj + j),
                                  pipeline_mode=pl.Buffered(1))
    return [part(0), part(1), part(2), pl.BlockSpec((None, 3, CONV_TILE), lambda j, i: (layer, 0, j))]


def _conv_in_prompt(xn, w_in, w_dw, layer, batch, seq):
    t, d = xn.shape
    tiles_per_seq = seq // TM_IN
    return pl.pallas_call(
        functools.partial(_conv_in_prompt_kernel, tiles_per_seq=tiles_per_seq),
        out_shape=[jax.ShapeDtypeStruct((t, d), BF16), jax.ShapeDtypeStruct((batch, 2, d), F32)],
        grid=(d // CONV_TILE, t // TM_IN),
        in_specs=[pl.BlockSpec((TM_IN, d), lambda j, i: (i, 0))] + _conv_weight_specs(d, layer),
        out_specs=[
            pl.BlockSpec((TM_IN, CONV_TILE), lambda j, i: (i, j)),
            pl.BlockSpec((1, 2, CONV_TILE), lambda j, i: (i // tiles_per_seq, 0, j)),
        ],
        scratch_shapes=[pltpu.VMEM((d, 3 * CONV_TILE), BF16), pltpu.VMEM((8, CONV_TILE), F32)],
        compiler_params=_params(("arbitrary", "arbitrary")),
        name="conv_in_prompt",
    )(xn, w_in, w_in, w_in, w_dw)


def _conv_in_sample_kernel(xn_ref, wb_ref, wc_ref, wx_ref, wdw_ref, hist_ref, z_ref, st_ref, w_ref, *, seq):
    _cast_conv_weights(wb_ref, wc_ref, wx_ref, w_ref)
    h = jnp.dot(xn_ref[...], w_ref[...], preferred_element_type=F32)
    hist = hist_ref[...]
    z, u3 = _gated_conv(h, hist[:, 0:1, :], hist[:, 1:2, :], wdw_ref[...], seq)
    z_ref[...] = z
    st_ref[...] = u3[:, seq - 2:, :]


def _conv_in_sample(xn, w_in, w_dw, layer, hist, seq):
    t, d = xn.shape
    nseq = TM_IN // seq
    hist_spec = pl.BlockSpec((nseq, 2, CONV_TILE), lambda j, i: (i, 0, j))
    return pl.pallas_call(
        functools.partial(_conv_in_sample_kernel, seq=seq),
        out_shape=[jax.ShapeDtypeStruct((t, d), BF16), jax.ShapeDtypeStruct(hist.shape, F32)],
        grid=(d // CONV_TILE, t // TM_IN),
        in_specs=[pl.BlockSpec((TM_IN, d), lambda j, i: (i, 0))] + _conv_weight_specs(d, layer) + [hist_spec],
        out_specs=[pl.BlockSpec((TM_IN, CONV_TILE), lambda j, i: (i, j)), hist_spec],
        scratch_shapes=[pltpu.VMEM((d, 3 * CONV_TILE), BF16)],
        compiler_params=_params(("arbitrary", "arbitrary")),
        name="conv_in_sample",
    )(xn, w_in, w_in, w_in, w_dw, hist)


def _qk(q, k):
    return lax.dot_general(q, k, (((1,), (1,)), ((), ())), preferred_element_type=F32)


def _band_attn_prompt_kernel(q_ref, k_ref, v_ref, bias_ref, o_ref):
    seq = q_ref.shape[0]
    n_groups = seq // QUERY_GROUP

    def key_range(m):
        q0 = m * QUERY_GROUP
        return max(0, q0 - PAST_WIN), q0 + QUERY_GROUP

    def scores(m):
        q0 = m * QUERY_GROUP
        lo, hi = key_range(m)
        col0 = lo - (q0 - PAST_WIN)
        return _qk(q_ref[q0:hi, :], k_ref[lo:hi, :]) + bias_ref[:, col0:col0 + hi - lo]

    s_next = scores(0)
    for m in range(n_groups):
        s = s_next
        if m + 1 < n_groups:
            s_next = scores(m + 1)
        lo, hi = key_range(m)
        p = jnp.exp2(s - jnp.max(s, axis=-1, keepdims=True))
        l = jnp.sum(p, axis=-1, keepdims=True)
        o = jnp.dot(p.astype(BF16), v_ref[lo:hi, :], preferred_element_type=F32) / l
        o_ref[m * QUERY_GROUP:hi, :] = o.astype(BF16)


def _band_attn_prompt(qkv_hm, bias, batch, seq):
    head = lambda off: pl.BlockSpec((None, seq, HEAD_DIM), lambda h, b: (off + h, b, 0))
    return pl.pallas_call(
        _band_attn_prompt_kernel,
        out_shape=jax.ShapeDtypeStruct((N_HEADS, batch * seq, HEAD_DIM), BF16),
        grid=(N_HEADS, batch),
        in_specs=[head(0), head(N_HEADS), head(2 * N_HEADS),
                  pl.BlockSpec((None,) + bias.shape[1:], lambda h, b: (h, 0, 0))],
        out_specs=head(0),
        compiler_params=_params(("arbitrary", "arbitrary")),
        name="band_attn_prompt",
    )(qkv_hm, qkv_hm, qkv_hm, bias)


def _band_attn_sample_kernel(qkv_ref, ck_ref, cv_ref, bias_c_ref, bias_n_ref, o_ref):
    for h in range(N_HEADS):
        q, kn, vn = qkv_ref[h], qkv_ref[N_HEADS + h], qkv_ref[2 * N_HEADS + h]
        head_rows = pl.ds(h, PAST_WIN, stride=N_HEADS)
        kc = ck_ref[head_rows, :].astype(BF16)
        vc = cv_ref[head_rows, :].astype(BF16)
        s_c = _qk(q, kc) + bias_c_ref[h]
        s_n = _qk(q, kn) + bias_n_ref[h]
        mx = jnp.maximum(jnp.max(s_c, axis=-1, keepdims=True), jnp.max(s_n, axis=-1, keepdims=True))
        p_c = jnp.exp2(s_c - mx)
        p_n = jnp.exp2(s_n - mx)
        l = jnp.sum(p_c, axis=-1, keepdims=True) + jnp.sum(p_n, axis=-1, keepdims=True)
        o = (jnp.dot(p_c.astype(BF16), vc, preferred_element_type=F32)
             + jnp.dot(p_n.astype(BF16), vn, preferred_element_type=F32)) / l
        o_ref[h] = o.astype(BF16)


def _band_attn_sample(qkv_hm, ck, cv, bias_c, bias_n, batch, seq):
    cache = pl.BlockSpec((None, PAST_WIN * N_HEADS, HEAD_DIM), lambda b: (b, 0, 0))
    const = lambda a: pl.BlockSpec(a.shape, lambda b: (0,) * a.ndim, pipeline_mode=pl.Buffered(1))
    return pl.pallas_call(
        _band_attn_sample_kernel,
        out_shape=jax.ShapeDtypeStruct((N_HEADS, batch * seq, HEAD_DIM), BF16),
        grid=(batch,),
        in_specs=[
            pl.BlockSpec((3 * N_HEADS, seq, HEAD_DIM), lambda b: (0, b, 0)),
            cache, cache,
            const(bias_c), const(bias_n),
        ],
        out_specs=pl.BlockSpec((N_HEADS, seq, HEAD_DIM), lambda b: (0, b, 0)),
        compiler_params=_params(("arbitrary",)),
        name="band_attn_sample",
    )(qkv_hm, ck, cv, bias_c, bias_n)


def _band_bias_table(rel_bias):
    nh = rel_bias.shape[0]
    n_keys = PAST_WIN + QUERY_GROUP
    period = n_keys + QUERY_GROUP
    n_far_past = PAST_WIN - MAX_REL + QUERY_GROUP - 1
    n_future = period - n_far_past - rel_bias.shape[1]
    f = jnp.concatenate([
        jnp.broadcast_to(rel_bias[:, -1:], (nh, n_far_past)),
        rel_bias[:, ::-1],
        jnp.broadcast_to(rel_bias[:, :1], (nh, n_future)),
    ], axis=1)
    f = jnp.roll(f, -(QUERY_GROUP - 1), axis=1)
    flat = jnp.tile(f, (1, QUERY_GROUP))[:, :QUERY_GROUP * (period - 1)]
    toeplitz = flat.reshape(nh, QUERY_GROUP, period - 1)[:, :, :n_keys]
    qi = jnp.arange(QUERY_GROUP)[:, None]
    kj = jnp.arange(n_keys)[None, :]
    band_lo = (qi // CHUNK) * CHUNK
    in_band = (kj >= band_lo) & (kj < band_lo + BAND)
    return jnp.where(in_band[None], toeplitz * LOG2_E, NEG_INF).astype(F32)


def _mem_attn_kernel(xn_ref, wq_ref, mk_ref, mv_ref, wo_ref, x_ref, gp_ref, gn_ref,
                     xo_ref, xno_ref, acc0_ref, acc1_ref):
    def attend():
        q = jnp.dot(xn_ref[...], wq_ref[...], preferred_element_type=F32) * (MEM_HEAD_DIM ** -0.5 * LOG2_E)
        n_streams = mk_ref.shape[0]
        rows_per_stream = q.shape[0] // n_streams
        streams = []
        for b in range(n_streams):
            rows = slice(b * rows_per_stream, (b + 1) * rows_per_stream)
            heads = []
            for h in range(MEM_HEADS):
                c = slice(h * MEM_HEAD_DIM, (h + 1) * MEM_HEAD_DIM)
                head_rows = pl.ds(h, N_MEM, stride=MEM_HEADS)
                s = _qk(q[rows, c].astype(BF16), mk_ref[b, head_rows, :].astype(BF16))
                p = jnp.exp2(s - jnp.max(s, axis=-1, keepdims=True))
                l = jnp.sum(p, axis=-1, keepdims=True)
                o = jnp.dot(p.astype(BF16), mv_ref[b, head_rows, :].astype(BF16), preferred_element_type=F32) / l
                heads.append(o.astype(BF16))
            streams.append(jnp.concatenate(heads, axis=1))
        return streams[0] if n_streams == 1 else jnp.concatenate(streams, axis=0)

    _skewed_step(attend, wo_ref, x_ref, gp_ref, gn_ref, xo_ref, xno_ref, acc0_ref, acc1_ref, 1.0)


def _mem_attn(xn, x, mk, mv, wq, wo, layer, g4, batch, tm):
    t, d = x.shape
    n = t // tm
    seq = t // batch
    streams_per_tile = max(1, tm // seq)
    assert tm * n == t and (seq % tm == 0 or tm % seq == 0)
    cur = lambda i: (jnp.minimum(i, n - 1), 0)
    prev = lambda i: (jnp.maximum(i - 1, 0), 0)
    mem_spec = pl.BlockSpec(
        (None, streams_per_tile, N_MEM * MEM_HEADS, MEM_HEAD_DIM),
        lambda i: (layer, (jnp.minimum(i, n - 1) * tm) // (streams_per_tile * seq), 0, 0))
    return pl.pallas_call(
        _mem_attn_kernel,
        out_shape=[jax.ShapeDtypeStruct((t, d), F32), jax.ShapeDtypeStruct((t, d), BF16)],
        grid=(n + 1,),
        in_specs=[
            pl.BlockSpec((tm, d), cur),
            _layer_spec((d, MEM_WIDTH), layer),
            mem_spec, mem_spec,
            _layer_spec((MEM_WIDTH, d), layer),
            pl.BlockSpec((tm, d), prev),
            _gain_spec(layer, 5),
            _gain_spec(layer, 6),
        ],
        out_specs=[pl.BlockSpec((tm, d), prev), pl.BlockSpec((tm, d), prev)],
        scratch_shapes=[pltpu.VMEM((tm, d), F32), pltpu.VMEM((tm, d), F32)],
        compiler_params=_params(("arbitrary",)),
        name="mem_attn",
    )(xn, wq, mk, mv, wo, x, g4, g4)


def _mem_project_kernel(mem_ref, g_ref, wf_ref, k_ref, v_ref, w_ref):
    @pl.when(pl.program_id(1) == 0)
    def _():
        w_ref[...] = wf_ref[...].astype(BF16)

    kv = jnp.dot(_rms(mem_ref[...], g_ref[...]).astype(BF16), w_ref[...], preferred_element_type=F32)
    tm = kv.shape[0]
    for h in range(MEM_HEADS):
        head_rows = pl.ds(h, tm, stride=MEM_HEADS)
        k_ref[head_rows, :] = kv[:, h * MEM_HEAD_DIM:(h + 1) * MEM_HEAD_DIM]
        v_ref[head_rows, :] = kv[:, MEM_WIDTH + h * MEM_HEAD_DIM:MEM_WIDTH + (h + 1) * MEM_HEAD_DIM]


def _mem_project(mem, g_mem, w_kv, tm):
    t, d = mem.shape
    depth = w_kv.shape[0]
    out = jax.ShapeDtypeStruct((depth, t * MEM_HEADS, MEM_HEAD_DIM), F32)
    out_spec = pl.BlockSpec((None, tm * MEM_HEADS, MEM_HEAD_DIM), lambda l, i: (l, i, 0))
    return pl.pallas_call(
        _mem_project_kernel,
        out_shape=[out, out],
        grid=(depth, t // tm),
        in_specs=[
            pl.BlockSpec((tm, d), lambda l, i: (i, 0)),
            pl.BlockSpec((None, 1, d), lambda l, i: (l, 0, 0)),
            pl.BlockSpec((None, d, 2 * MEM_WIDTH), lambda l, i: (l, 0, 0)),
        ],
        out_specs=[out_spec, out_spec],
        scratch_shapes=[pltpu.VMEM((d, 2 * MEM_WIDTH), BF16)],
        compiler_params=_params(("arbitrary", "arbitrary")),
        name="mem_project",
    )(mem, g_mem, w_kv)


def _trunk(x, mem_k, mem_v, weights, *, batch, seq, conv_hist, band_cache, bias_tables, ffn_out_bf16=None):
    prompt = conv_hist is None
    (g4, w_ffn1_in, w_ffn1_out, w_ffn2_in, w_ffn2_out, w_conv_in, w_conv_dw, w_conv_out,
     w_attn_qkv, w_attn_o, w_mem_q, w_mem_o) = weights
    depth = g4.shape[0]
    tm_mem = TM_MEM if prompt else TM_MEM_SAMPLE
    cast_here = ffn_out_bf16 is None
    if cast_here:
        ffn_out_bf16 = []

    def ffn_in(xn, w_in, w_out, layer, which):
        if cast_here:
            a, w_out_bf16 = _ffn_in(xn, w_in, layer, w_out)
            ffn_out_bf16.append(w_out_bf16)
            return a, w_out_bf16
        return _ffn_in(xn, w_in, layer), ffn_out_bf16[2 * layer + which]

    xn = _rms_cast(x, g4)
    conv_state = band_k = band_v = None
    for i in range(depth):
        m = i // 2
        a, w_out = ffn_in(xn, w_ffn1_in, w_ffn1_out, i, 0)
        x, xn = _mm_res_norm(a, w_out, 0, x, g4, (i, 1), (i, 2), 0.5, TM_FFN_OUT)
        if i % 2 == 0:
            if prompt:
                z, conv_state = _conv_in_prompt(xn, w_conv_in, w_conv_dw, m, batch, seq)
            else:
                z, conv_state = _conv_in_sample(xn, w_conv_in, w_conv_dw, m, conv_hist, seq)
            x, xn = _mm_res_norm(z, w_conv_out, m, x, g4, (i, 3), (i, 4), 1.0, TM_MIX_OUT)
        else:
            if prompt:
                qkv_hm, band_k, band_v = _qkv(xn, w_attn_qkv, m, seq // TM_QKV)
                o = _band_attn_prompt(qkv_hm, bias_tables[0], batch, seq)
            else:
                qkv_hm, band_k, band_v = _qkv(xn, w_attn_qkv, m, 1)
                o = _band_attn_sample(qkv_hm, band_cache[0], band_cache[1], bias_tables[1], bias_tables[2],
                                      batch, seq)
            x, xn = _mm_res_norm(o, w_attn_o, m, x, g4, (i, 3), (i, 4), 1.0, TM_MIX_OUT)
        x, xn = _mem_attn(xn, x, mem_k, mem_v, w_mem_q, w_mem_o, i, g4, batch, tm_mem)
        a, w_out = ffn_in(xn, w_ffn2_in, w_ffn2_out, i, 1)
        last = i == depth - 1
        x, xn = _mm_res_norm(a, w_out, 0, x, g4, (i, 7), ((i + 1) % depth, 0), 0.5, TM_FFN_OUT,
                             emit_xn=not last)
    return x, conv_state, band_k, band_v, ffn_out_bf16


def kernel(x_prompt, x_sample, state_conv, cache_band_k, cache_band_v, cache_mem_k, cache_mem_v, mem_prompt, g_norm, g_mem, w_ffn1_in, w_ffn1_out, w_ffn2_in, w_ffn2_out, w_conv_in, w_conv_dw, w_conv_out, w_attn_qkv, rel_bias, w_attn_o, w_mem_q, w_mem_kv, w_mem_o):
    batch, seq, d = x_prompt.shape
    dec_batch, dec_seq, _ = x_sample.shape
    depth = g_norm.shape[0]
    assert depth == 2 and state_conv.shape[0] == 1 and cache_band_k.shape[0] == 1
    assert cache_band_k.shape[2] == PAST_WIN and dec_seq == CHUNK and seq % QUERY_GROUP == 0
    assert seq % TM_IN == 0 and seq >= PAST_WIN and TM_IN % dec_seq == 0 and TM_QKV % dec_seq == 0

    weights = (
        g_norm[:, :, None, :],
        w_ffn1_in, w_ffn1_out, w_ffn2_in, w_ffn2_out,
        w_conv_in, w_conv_dw, w_conv_out.astype(BF16),
        w_attn_qkv.astype(BF16), w_attn_o.astype(BF16), w_mem_q.astype(BF16), w_mem_o.astype(BF16),
    )
    table = _band_bias_table(rel_bias[0])
    bias_tables = (table, table[:, :CHUNK, :PAST_WIN], table[:, :CHUNK, PAST_WIN:BAND])

    mem_rows = N_MEM * MEM_HEADS
    mem_k_p, mem_v_p = _mem_project(mem_prompt.reshape(batch * N_MEM, d), g_mem[:, None, :], w_mem_kv, 512)

    y_p, conv_p, bk_p, bv_p, ffn_out_bf16 = _trunk(
        x_prompt.reshape(batch * seq, d),
        mem_k_p.reshape(depth, batch, mem_rows, MEM_HEAD_DIM),
        mem_v_p.reshape(depth, batch, mem_rows, MEM_HEAD_DIM), weights,
        batch=batch, seq=seq, conv_hist=None, band_cache=None, bias_tables=bias_tables)

    y_s, conv_s, bk_s, bv_s, _ = _trunk(
        x_sample.reshape(dec_batch * dec_seq, d),
        cache_mem_k.reshape(depth, dec_batch, mem_rows, MEM_HEAD_DIM),
        cache_mem_v.reshape(depth, dec_batch, mem_rows, MEM_HEAD_DIM), weights,
        batch=dec_batch, seq=dec_seq, conv_hist=state_conv[0],
        band_cache=(cache_band_k.reshape(dec_batch, PAST_WIN * N_HEADS, HEAD_DIM),
                    cache_band_v.reshape(dec_batch, PAST_WIN * N_HEADS, HEAD_DIM)),
        bias_tables=bias_tables, ffn_out_bf16=ffn_out_bf16)

    mem_shape = (depth, batch, N_MEM, MEM_HEADS, MEM_HEAD_DIM)
    band_p_shape = (1, batch, PAST_WIN, N_HEADS, HEAD_DIM)
    band_s_shape = (1, dec_batch, dec_seq, N_HEADS, HEAD_DIM)
    return (y_p.reshape(batch, seq, d), y_s.reshape(dec_batch, dec_seq, d),
            conv_p[None], bk_p.reshape(band_p_shape), bv_p.reshape(band_p_shape),
            mem_k_p.reshape(mem_shape), mem_v_p.reshape(mem_shape),
            conv_s[None], bk_s.reshape(band_s_shape), bv_s.reshape(band_s_shape))
```

```python
import functools

import jax
import jax.numpy as jnp
from jax import lax
from jax.experimental import pallas as pl
from jax.experimental.pallas import tpu as pltpu

F32 = jnp.float32
BF16 = jnp.bfloat16

D_MODEL = 2048
D_FF = 5632
N_HEADS = 16
HEAD_DIM = 128
CHUNK = 64
PAST_WIN = 512
BAND = PAST_WIN + CHUNK
MAX_REL = 128
N_MEM = 256
MEM_HEADS = 4
MEM_HEAD_DIM = 128
MEM_WIDTH = MEM_HEADS * MEM_HEAD_DIM
EPS = 1e-6
NEG_INF = -1e30
LOG2_E = 1.4426950408889634

V7X_VMEM_LIMIT_BYTES = 58 * 1024 * 1024
BF16_SUBLANES = 16
LANES = 128
EPILOGUE_PIECES = 4
FF_TILE = 512
CONV_TILE = 512
TM_QKV = PAST_WIN
QUERY_GROUP = 4 * CHUNK
TM_IN = 1024
TM_FFN_IN = 2048
TM_FFN_OUT = 256
TM_MIX_OUT = 512
TM_MEM = 512
TM_MEM_SAMPLE = 256


def _params(sem):
    return pltpu.CompilerParams(dimension_semantics=sem, vmem_limit_bytes=V7X_VMEM_LIMIT_BYTES)


def _layer_spec(shape, layer):
    zeros = (0,) * len(shape)
    return pl.BlockSpec((None,) + tuple(shape), lambda *_: (layer,) + zeros, pipeline_mode=pl.Buffered(1))


def _gain_spec(layer, n):
    return pl.BlockSpec((None, None, 1, D_MODEL), lambda *_: (layer, n, 0, 0), pipeline_mode=pl.Buffered(1))


def _rms(x, g):
    return x * lax.rsqrt(jnp.mean(x * x, axis=-1, keepdims=True) + EPS) * g


def _rms_cast_kernel(x_ref, g_ref, o_ref):
    o_ref[...] = _rms(x_ref[...], g_ref[...]).astype(BF16)


def _rms_cast(x, g4):
    t, d = x.shape
    return pl.pallas_call(
        _rms_cast_kernel,
        out_shape=jax.ShapeDtypeStruct((t, d), BF16),
        grid=(t // TM_IN,),
        in_specs=[pl.BlockSpec((TM_IN, d), lambda i: (i, 0)), _gain_spec(0, 0)],
        out_specs=pl.BlockSpec((TM_IN, d), lambda i: (i, 0)),
        compiler_params=_params(("arbitrary",)),
        name="rms_cast",
    )(x, g4)


def _res_norm_epilogue(acc_ref, x_ref, gp, gn, xo_ref, xno_ref, row0, n_rows):
    rows = BF16_SUBLANES
    xnew = None
    for r in range(row0, row0 + n_rows, rows):
        sl = slice(r, r + rows)
        xnew = x_ref[sl, :] + _rms(acc_ref[sl, :], gp)
        xo_ref[sl, :] = xnew
        if xno_ref is not None:
            xno_ref[sl, :] = _rms(xnew, gn).astype(BF16)
    return xnew[:, :LANES]


def _zero_from(x):
    sixteen = jnp.uint32(16)
    bits = lax.bitcast_convert_type(x, jnp.uint32)
    return lax.shift_right_logical(lax.shift_right_logical(bits, sixteen), sixteen).astype(F32)


def _after(lhs, zero):
    r = BF16_SUBLANES
    first = (lhs[:r, :LANES].astype(F32) + zero).astype(lhs.dtype)
    top = jnp.concatenate([first, lhs[:r, LANES:]], axis=1)
    return jnp.concatenate([top, lhs[r:, :]], axis=0)


def _skewed_step(load_lhs, w_ref, x_ref, gp_ref, gn_ref, xo_ref, xno_ref, acc0_ref, acc1_ref, scale):
    i = pl.program_id(0)
    tm, d = acc0_ref.shape
    rows = tm // EPILOGUE_PIECES
    cols = d // EPILOGUE_PIECES

    @pl.when(i == 0)
    def _():
        acc1_ref[...] = jnp.zeros_like(acc1_ref)

    def step(prev_ref, cur_ref):
        gp = gp_ref[...] * scale
        gn = gn_ref[...]
        lhs = load_lhs()
        for k in range(EPILOGUE_PIECES):
            c = slice(k * cols, (k + 1) * cols)
            cur_ref[:, c] = jnp.dot(lhs, w_ref[:, c], preferred_element_type=F32)
            done = _res_norm_epilogue(prev_ref, x_ref, gp, gn, xo_ref, xno_ref, k * rows, rows)
            lhs = _after(lhs, _zero_from(done))

    @pl.when(i % 2 == 0)
    def _():
        step(acc1_ref, acc0_ref)

    @pl.when(i % 2 == 1)
    def _():
        step(acc0_ref, acc1_ref)


def _load_rows(a_ref):
    if len(a_ref.shape) == 2:
        return a_ref[...]
    return jnp.concatenate([a_ref[h] for h in range(a_ref.shape[0])], axis=1)


def _mm_res_norm_kernel(a_ref, w_ref, x_ref, gp_ref, gn_ref, xo_ref, *rest, scale, emit_xn):
    xno_ref, acc0_ref, acc1_ref = rest if emit_xn else (None,) + rest
    _skewed_step(lambda: _load_rows(a_ref), w_ref, x_ref, gp_ref, gn_ref, xo_ref, xno_ref,
                 acc0_ref, acc1_ref, scale)


def _mm_res_norm(a, w, w_layer, x, g4, post_gain, next_gain, scale, tm, emit_xn=True):
    t, d = x.shape
    k = w.shape[1]
    n = t // tm
    cur = lambda i: jnp.minimum(i, n - 1)
    prev = lambda i: (jnp.maximum(i - 1, 0), 0)
    if a.ndim == 2:
        a_spec = pl.BlockSpec((tm, k), lambda i: (cur(i), 0))
    else:
        a_spec = pl.BlockSpec((a.shape[0], tm, a.shape[2]), lambda i: (0, cur(i), 0))
    out_shape = [jax.ShapeDtypeStruct((t, d), F32)]
    out_specs = [pl.BlockSpec((tm, d), prev)]
    if emit_xn:
        out_shape.append(jax.ShapeDtypeStruct((t, d), BF16))
        out_specs.append(pl.BlockSpec((tm, d), prev))
    outs = pl.pallas_call(
        functools.partial(_mm_res_norm_kernel, scale=scale, emit_xn=emit_xn),
        out_shape=out_shape,
        grid=(n + 1,),
        in_specs=[
            a_spec,
            _layer_spec((k, d), w_layer),
            pl.BlockSpec((tm, d), prev),
            _gain_spec(*post_gain),
            _gain_spec(*next_gain),
        ],
        out_specs=out_specs,
        scratch_shapes=[pltpu.VMEM((tm, d), F32), pltpu.VMEM((tm, d), F32)],
        compiler_params=_params(("arbitrary",)),
        name="mm_res_norm",
    )(a, w, x, g4, g4)
    return (outs[0], outs[1]) if emit_xn else (outs[0], None)


def _ffn_in_kernel(xn_ref, wg_ref, wu_ref, *rest, cast_w_out):
    if cast_w_out:
        wo_f32_ref, o_ref, wo_bf16_ref, w_ref = rest
        wo_bf16_ref[...] = wo_f32_ref[...].astype(BF16)
    else:
        o_ref, w_ref = rest

    @pl.when(pl.program_id(1) == 0)
    def _():
        w_ref[:, :FF_TILE] = wg_ref[...].astype(BF16)
        w_ref[:, FF_TILE:] = wu_ref[...].astype(BF16)

    half = xn_ref.shape[0] // 2
    for r in (slice(0, half), slice(half, 2 * half)):
        h = jnp.dot(xn_ref[r, :], w_ref[...], preferred_element_type=F32)
        o_ref[r, :] = (jax.nn.silu(h[:, :FF_TILE]) * h[:, FF_TILE:]).astype(BF16)


def _ffn_in(xn, w_in, layer, w_out=None):
    t, d = xn.shape
    nj = D_FF // FF_TILE
    ni = t // TM_FFN_IN
    in_specs = [
        pl.BlockSpec((TM_FFN_IN, d), lambda j, i: (i, 0)),
        pl.BlockSpec((None, d, FF_TILE), lambda j, i: (layer, 0, j)),
        pl.BlockSpec((None, d, FF_TILE), lambda j, i: (layer, 0, nj + j)),
    ]
    out_shape = [jax.ShapeDtypeStruct((t, D_FF), BF16)]
    out_specs = [pl.BlockSpec((TM_FFN_IN, FF_TILE), lambda j, i: (i, j))]
    operands = [xn, w_in, w_in]
    if w_out is not None:
        rows = D_FF // (nj * ni)
        assert rows * nj * ni == D_FF and rows % BF16_SUBLANES == 0
        in_specs.append(pl.BlockSpec((None, rows, d), lambda j, i: (layer, j * ni + i, 0)))
        out_shape.append(jax.ShapeDtypeStruct((1, D_FF, d), BF16))
        out_specs.append(pl.BlockSpec((None, rows, d), lambda j, i: (0, j * ni + i, 0)))
        operands.append(w_out)
    outs = pl.pallas_call(
        functools.partial(_ffn_in_kernel, cast_w_out=w_out is not None),
        out_shape=out_shape,
        grid=(nj, ni),
        in_specs=in_specs,
        out_specs=out_specs,
        scratch_shapes=[pltpu.VMEM((d, 2 * FF_TILE), BF16)],
        compiler_params=_params(("arbitrary", "arbitrary")),
        name="ffn_in",
    )(*operands)
    return outs if w_out is not None else outs[0]


def _qkv_kernel(xn_ref, w_ref, hm_ref, k_ref, v_ref, *, keep_every):
    j, i = pl.program_id(0), pl.program_id(1)
    acc = jnp.dot(xn_ref[...], w_ref[...], preferred_element_type=F32)
    heads = [acc[:, h * HEAD_DIM:(h + 1) * HEAD_DIM] for h in range(N_HEADS)]
    hm_scale = jnp.where(j == 0, HEAD_DIM ** -0.5 * LOG2_E, 1.0)
    for h in range(N_HEADS):
        hm_ref[h] = (heads[h] * hm_scale).astype(BF16)

    def keep(ref):
        for h in range(N_HEADS):
            ref[pl.ds(h, acc.shape[0], stride=N_HEADS), :] = heads[h]

    kept_tile = i % keep_every == keep_every - 1

    @pl.when((j == 1) & kept_tile)
    def _():
        keep(k_ref)

    @pl.when((j == 2) & kept_tile)
    def _():
        keep(v_ref)


def _qkv(xn, w_qkv, layer, keep_every):
    t, d = xn.shape
    n_keep = t // (TM_QKV * keep_every)
    kept = jax.ShapeDtypeStruct((n_keep, TM_QKV * N_HEADS, HEAD_DIM), F32)

    def k_index(j, i):
        return (jnp.where(j < 1, 0, jnp.where(j > 1, n_keep - 1, i // keep_every)), 0, 0)

    def v_index(j, i):
        return (jnp.where(j < 2, 0, i // keep_every), 0, 0)

    return pl.pallas_call(
        functools.partial(_qkv_kernel, keep_every=keep_every),
        out_shape=[jax.ShapeDtypeStruct((3 * N_HEADS, t, HEAD_DIM), BF16), kept, kept],
        grid=(3, t // TM_QKV),
        in_specs=[
            pl.BlockSpec((TM_QKV, d), lambda j, i: (i, 0)),
            pl.BlockSpec((None, d, D_MODEL), lambda j, i: (layer, 0, j), pipeline_mode=pl.Buffered(1)),
        ],
        out_specs=[
            pl.BlockSpec((N_HEADS, TM_QKV, HEAD_DIM), lambda j, i: (j, i, 0)),
            pl.BlockSpec((None, TM_QKV * N_HEADS, HEAD_DIM), k_index),
            pl.BlockSpec((None, TM_QKV * N_HEADS, HEAD_DIM), v_index),
        ],
        compiler_params=_params(("arbitrary", "arbitrary")),
        name="qkv",
    )(xn, w_qkv)


def _cast_conv_weights(wb_ref, wc_ref, wx_ref, w_ref):
    @pl.when(pl.program_id(1) == 0)
    def _():
        for c in range(CONV_TILE // LANES):
            lane = slice(c * LANES, (c + 1) * LANES)
            for part, ref in enumerate((wb_ref, wc_ref, wx_ref)):
                w_ref[:, (3 * c + part) * LANES:(3 * c + part + 1) * LANES] = ref[:, lane].astype(BF16)


def _gated_conv(h, hist0, hist1, wdw, seq):
    tm = h.shape[0]
    nseq = tm // seq
    row = lax.broadcasted_iota(jnp.int32, (nseq, seq, LANES), 1)
    zs, us = [], []
    for c in range(CONV_TILE // LANES):
        lane = slice(c * LANES, (c + 1) * LANES)
        bg, cg, xv = (h[:, (3 * c + part) * LANES:(3 * c + part + 1) * LANES] for part in range(3))
        u = cg * xv
        u3 = u.reshape(nseq, seq, LANES)
        prev1 = pltpu.roll(u, 1, axis=0).reshape(u3.shape)
        prev2 = pltpu.roll(u, 2, axis=0).reshape(u3.shape)
        prev1 = jnp.where(row == 0, hist1[:, :, lane], prev1)
        prev2 = jnp.where(row == 0, hist0[:, :, lane], jnp.where(row == 1, hist1[:, :, lane], prev2))
        y = wdw[0:1, lane] * prev2 + wdw[1:2, lane] * prev1 + wdw[2:3, lane] * u3
        zs.append((bg.reshape(u3.shape) * y).astype(BF16).reshape(tm, LANES))
        us.append(u3)
    return jnp.concatenate(zs, axis=1), jnp.concatenate(us, axis=2)


def _conv_in_prompt_kernel(xn_ref, wb_ref, wc_ref, wx_ref, wdw_ref, z_ref, st_ref, w_ref, carry_ref,
                           *, tiles_per_seq):
    tm = xn_ref.shape[0]
    _cast_conv_weights(wb_ref, wc_ref, wx_ref, w_ref)

    @pl.when(pl.program_id(1) % tiles_per_seq == 0)
    def _():
        carry_ref[...] = jnp.zeros_like(carry_ref)

    h = jnp.dot(xn_ref[...], w_ref[...], preferred_element_type=F32)
    hist0 = carry_ref[0:1, :].reshape(1, 1, CONV_TILE)
    hist1 = carry_ref[1:2, :].reshape(1, 1, CONV_TILE)
    z, u3 = _gated_conv(h, hist0, hist1, wdw_ref[...], tm)
    z_ref[...] = z
    last = u3[0, tm - 2:, :]
    carry_ref[0:2, :] = last
    st_ref[0] = last


def _conv_weight_specs(d, layer):
    nj = d // CONV_TILE
    part = lambda p: pl.BlockSpec((None, d, CONV_TILE), lambda j, i: (layer, 0, p * nj + j),
                                  pipeline_mode=pl.Buffered(1))
    return [part(0), part(1), part(2), pl.BlockSpec((None, 3, CONV_TILE), lambda j, i: (layer, 0, j))]


def _conv_in_prompt(xn, w_in, w_dw, layer, batch, seq):
    t, d = xn.shape
    tiles_per_seq = seq // TM_IN
    return pl.pallas_call(
        functools.partial(_conv_in_prompt_kernel, tiles_per_seq=tiles_per_seq),
        out_shape=[jax.ShapeDtypeStruct((t, d), BF16), jax.ShapeDtypeStruct((batch, 2, d), F32)],
        grid=(d // CONV_TILE, t // TM_IN),
        in_specs=[pl.BlockSpec((TM_IN, d), lambda j, i: (i, 0))] + _conv_weight_specs(d, layer),
        out_specs=[
            pl.BlockSpec((TM_IN, CONV_TILE), lambda j, i: (i, j)),
            pl.BlockSpec((1, 2, CONV_TILE), lambda j, i: (i // tiles_per_seq, 0, j)),
        ],
        scratch_shapes=[pltpu.VMEM((d, 3 * CONV_TILE), BF16), pltpu.VMEM((8, CONV_TILE), F32)],
        compiler_params=_params(("arbitrary", "arbitrary")),
        name="conv_in_prompt",
    )(xn, w_in, w_in, w_in, w_dw)


def _conv_in_sample_kernel(xn_ref, wb_ref, wc_ref, wx_ref, wdw_ref, hist_ref, z_ref, st_ref, w_ref, *, seq):
    _cast_conv_weights(wb_ref, wc_ref, wx_ref, w_ref)
    h = jnp.dot(xn_ref[...], w_ref[...], preferred_element_type=F32)
    hist = hist_ref[...]
    z, u3 = _gated_conv(h, hist[:, 0:1, :], hist[:, 1:2, :], wdw_ref[...], seq)
    z_ref[...] = z
    st_ref[...] = u3[:, seq - 2:, :]


def _conv_in_sample(xn, w_in, w_dw, layer, hist, seq):
    t, d = xn.shape
    nseq = TM_IN // seq
    hist_spec = pl.BlockSpec((nseq, 2, CONV_TILE), lambda j, i: (i, 0, j))
    return pl.pallas_call(
        functools.partial(_conv_in_sample_kernel, seq=seq),
        out_shape=[jax.ShapeDtypeStruct((t, d), BF16), jax.ShapeDtypeStruct(hist.shape, F32)],
        grid=(d // CONV_TILE, t // TM_IN),
        in_specs=[pl.BlockSpec((TM_IN, d), lambda j, i: (i, 0))] + _conv_weight_specs(d, layer) + [hist_spec],
        out_specs=[pl.BlockSpec((TM_IN, CONV_TILE), lambda j, i: (i, j)), hist_spec],
        scratch_shapes=[pltpu.VMEM((d, 3 * CONV_TILE), BF16)],
        compiler_params=_params(("arbitrary", "arbitrary")),
        name="conv_in_sample",
    )(xn, w_in, w_in, w_in, w_dw, hist)


def _qk(q, k):
    return lax.dot_general(q, k, (((1,), (1,)), ((), ())), preferred_element_type=F32)


def _band_attn_prompt_kernel(q_ref, k_ref, v_ref, bias_ref, o_ref):
    seq = q_ref.shape[0]
    n_groups = seq // QUERY_GROUP

    def key_range(m):
        q0 = m * QUERY_GROUP
        return max(0, q0 - PAST_WIN), q0 + QUERY_GROUP

    def scores(m):
        q0 = m * QUERY_GROUP
        lo, hi = key_range(m)
        col0 = lo - (q0 - PAST_WIN)
        return _qk(q_ref[q0:hi, :], k_ref[lo:hi, :]) + bias_ref[:, col0:col0 + hi - lo]

    s_next = scores(0)
    for m in range(n_groups):
        s = s_next
        if m + 1 < n_groups:
            s_next = scores(m + 1)
        lo, hi = key_range(m)
        p = jnp.exp2(s - jnp.max(s, axis=-1, keepdims=True))
        l = jnp.sum(p, axis=-1, keepdims=True)
        o = jnp.dot(p.astype(BF16), v_ref[lo:hi, :], preferred_element_type=F32) / l
        o_ref[m * QUERY_GROUP:hi, :] = o.astype(BF16)


def _band_attn_prompt(qkv_hm, bias, batch, seq):
    head = lambda off: pl.BlockSpec((None, seq, HEAD_DIM), lambda h, b: (off + h, b, 0))
    return pl.pallas_call(
        _band_attn_prompt_kernel,
        out_shape=jax.ShapeDtypeStruct((N_HEADS, batch * seq, HEAD_DIM), BF16),
        grid=(N_HEADS, batch),
        in_specs=[head(0), head(N_HEADS), head(2 * N_HEADS),
                  pl.BlockSpec((None,) + bias.shape[1:], lambda h, b: (h, 0, 0))],
        out_specs=head(0),
        compiler_params=_params(("arbitrary", "arbitrary")),
        name="band_attn_prompt",
    )(qkv_hm, qkv_hm, qkv_hm, bias)


def _band_attn_sample_kernel(qkv_ref, ck_ref, cv_ref, bias_c_ref, bias_n_ref, o_ref):
    k_heads = pltpu.einshape("thd->htd", ck_ref[...]).astype(BF16)
    v_heads = pltpu.einshape("thd->htd", cv_ref[...]).astype(BF16)
    for h in range(N_HEADS):
        q, kn, vn = qkv_ref[h], qkv_ref[N_HEADS + h], qkv_ref[2 * N_HEADS + h]
        kc = k_heads[h]
        vc = v_heads[h]
        s_c = _qk(q, kc) + bias_c_ref[h]
        s_n = _qk(q, kn) + bias_n_ref[h]
        mx = jnp.maximum(jnp.max(s_c, axis=-1, keepdims=True), jnp.max(s_n, axis=-1, keepdims=True))
        p_c = jnp.exp2(s_c - mx)
        p_n = jnp.exp2(s_n - mx)
        l = jnp.sum(p_c, axis=-1, keepdims=True) + jnp.sum(p_n, axis=-1, keepdims=True)
        o = (jnp.dot(p_c.astype(BF16), vc, preferred_element_type=F32)
             + jnp.dot(p_n.astype(BF16), vn, preferred_element_type=F32)) / l
        o_ref[h] = o.astype(BF16)


def _band_attn_sample(qkv_hm, ck, cv, bias_c, bias_n, batch, seq):
    cache = pl.BlockSpec((None, PAST_WIN, N_HEADS, HEAD_DIM), lambda b: (b, 0, 0, 0))
    const = lambda a: pl.BlockSpec(a.shape, lambda b: (0,) * a.ndim, pipeline_mode=pl.Buffered(1))
    return pl.pallas_call(
        _band_attn_sample_kernel,
        out_shape=jax.ShapeDtypeStruct((N_HEADS, batch * seq, HEAD_DIM), BF16),
        grid=(batch,),
        in_specs=[
            pl.BlockSpec((3 * N_HEADS, seq, HEAD_DIM), lambda b: (0, b, 0)),
            cache, cache,
            const(bias_c), const(bias_n),
        ],
        out_specs=pl.BlockSpec((N_HEADS, seq, HEAD_DIM), lambda b: (0, b, 0)),
        compiler_params=_params(("arbitrary",)),
        name="band_attn_sample",
    )(qkv_hm, ck, cv, bias_c, bias_n)


def _band_bias_table(rel_bias):
    nh = rel_bias.shape[0]
    n_keys = PAST_WIN + QUERY_GROUP
    period = n_keys + QUERY_GROUP
    n_far_past = PAST_WIN - MAX_REL + QUERY_GROUP - 1
    n_future = period - n_far_past - rel_bias.shape[1]
    f = jnp.concatenate([
        jnp.broadcast_to(rel_bias[:, -1:], (nh, n_far_past)),
        rel_bias[:, ::-1],
        jnp.broadcast_to(rel_bias[:, :1], (nh, n_future)),
    ], axis=1)
    f = jnp.roll(f, -(QUERY_GROUP - 1), axis=1)
    flat = jnp.tile(f, (1, QUERY_GROUP))[:, :QUERY_GROUP * (period - 1)]
    toeplitz = flat.reshape(nh, QUERY_GROUP, period - 1)[:, :, :n_keys]
    qi = jnp.arange(QUERY_GROUP)[:, None]
    kj = jnp.arange(n_keys)[None, :]
    band_lo = (qi // CHUNK) * CHUNK
    in_band = (kj >= band_lo) & (kj < band_lo + BAND)
    return jnp.where(in_band[None], toeplitz * LOG2_E, NEG_INF).astype(F32)


def _mem_attn_kernel(xn_ref, wq_ref, mk_ref, mv_ref, wo_ref, x_ref, gp_ref, gn_ref,
                     xo_ref, xno_ref, acc0_ref, acc1_ref):
    def attend():
        q = jnp.dot(xn_ref[...], wq_ref[...], preferred_element_type=F32) * (MEM_HEAD_DIM ** -0.5 * LOG2_E)
        n_streams = mk_ref.shape[0]
        rows_per_stream = q.shape[0] // n_streams
        streams = []
        for b in range(n_streams):
            rows = slice(b * rows_per_stream, (b + 1) * rows_per_stream)
            heads = []
            for h in range(MEM_HEADS):
                c = slice(h * MEM_HEAD_DIM, (h + 1) * MEM_HEAD_DIM)
                head_rows = pl.ds(h, N_MEM, stride=MEM_HEADS)
                s = _qk(q[rows, c].astype(BF16), mk_ref[b, head_rows, :].astype(BF16))
                p = jnp.exp2(s - jnp.max(s, axis=-1, keepdims=True))
                l = jnp.sum(p, axis=-1, keepdims=True)
                o = jnp.dot(p.astype(BF16), mv_ref[b, head_rows, :].astype(BF16), preferred_element_type=F32) / l
                heads.append(o.astype(BF16))
            streams.append(jnp.concatenate(heads, axis=1))
        return streams[0] if n_streams == 1 else jnp.concatenate(streams, axis=0)

    _skewed_step(attend, wo_ref, x_ref, gp_ref, gn_ref, xo_ref, xno_ref, acc0_ref, acc1_ref, 1.0)


def _mem_attn(xn, x, mk, mv, wq, wo, layer, g4, batch, tm):
    t, d = x.shape
    n = t // tm
    seq = t // batch
    streams_per_tile = max(1, tm // seq)
    assert tm * n == t and (seq % tm == 0 or tm % seq == 0)
    cur = lambda i: (jnp.minimum(i, n - 1), 0)
    prev = lambda i: (jnp.maximum(i - 1, 0), 0)
    mem_spec = pl.BlockSpec(
        (None, streams_per_tile, N_MEM * MEM_HEADS, MEM_HEAD_DIM),
        lambda i: (layer, (jnp.minimum(i, n - 1) * tm) // (streams_per_tile * seq), 0, 0))
    return pl.pallas_call(
        _mem_attn_kernel,
        out_shape=[jax.ShapeDtypeStruct((t, d), F32), jax.ShapeDtypeStruct((t, d), BF16)],
        grid=(n + 1,),
        in_specs=[
            pl.BlockSpec((tm, d), cur),
            _layer_spec((d, MEM_WIDTH), layer),
            mem_spec, mem_spec,
            _layer_spec((MEM_WIDTH, d), layer),
            pl.BlockSpec((tm, d), prev),
            _gain_spec(layer, 5),
            _gain_spec(layer, 6),
        ],
        out_specs=[pl.BlockSpec((tm, d), prev), pl.BlockSpec((tm, d), prev)],
        scratch_shapes=[pltpu.VMEM((tm, d), F32), pltpu.VMEM((tm, d), F32)],
        compiler_params=_params(("arbitrary",)),
        name="mem_attn",
    )(xn, wq, mk, mv, wo, x, g4, g4)


def _mem_project_kernel(mem_ref, g_ref, wf_ref, k_ref, v_ref, w_ref):
    @pl.when(pl.program_id(1) == 0)
    def _():
        w_ref[...] = wf_ref[...].astype(BF16)

    kv = jnp.dot(_rms(mem_ref[...], g_ref[...]).astype(BF16), w_ref[...], preferred_element_type=F32)
    tm = kv.shape[0]
    for h in range(MEM_HEADS):
        head_rows = pl.ds(h, tm, stride=MEM_HEADS)
        k_ref[head_rows, :] = kv[:, h * MEM_HEAD_DIM:(h + 1) * MEM_HEAD_DIM]
        v_ref[head_rows, :] = kv[:, MEM_WIDTH + h * MEM_HEAD_DIM:MEM_WIDTH + (h + 1) * MEM_HEAD_DIM]


def _mem_project(mem, g_mem, w_kv, tm):
    t, d = mem.shape
    depth = w_kv.shape[0]
    out = jax.ShapeDtypeStruct((depth, t * MEM_HEADS, MEM_HEAD_DIM), F32)
    out_spec = pl.BlockSpec((None, tm * MEM_HEADS, MEM_HEAD_DIM), lambda l, i: (l, i, 0))
    return pl.pallas_call(
        _mem_project_kernel,
        out_shape=[out, out],
        grid=(depth, t // tm),
        in_specs=[
            pl.BlockSpec((tm, d), lambda l, i: (i, 0)),
            pl.BlockSpec((None, 1, d), lambda l, i: (l, 0, 0)),
            pl.BlockSpec((None, d, 2 * MEM_WIDTH), lambda l, i: (l, 0, 0)),
        ],
        out_specs=[out_spec, out_spec],
        scratch_shapes=[pltpu.VMEM((d, 2 * MEM_WIDTH), BF16)],
        compiler_params=_params(("arbitrary", "arbitrary")),
        name="mem_project",
    )(mem, g_mem, w_kv)


def _trunk(x, mem_k, mem_v, weights, *, batch, seq, conv_hist, band_cache, bias_tables, ffn_out_bf16=None):
    prompt = conv_hist is None
    (g4, w_ffn1_in, w_ffn1_out, w_ffn2_in, w_ffn2_out, w_conv_in, w_conv_dw, w_conv_out,
     w_attn_qkv, w_attn_o, w_mem_q, w_mem_o) = weights
    depth = g4.shape[0]
    tm_mem = TM_MEM if prompt else TM_MEM_SAMPLE
    cast_here = ffn_out_bf16 is None
    if cast_here:
        ffn_out_bf16 = []

    def ffn_in(xn, w_in, w_out, layer, which):
        if cast_here:
            a, w_out_bf16 = _ffn_in(xn, w_in, layer, w_out)
            ffn_out_bf16.append(w_out_bf16)
            return a, w_out_bf16
        return _ffn_in(xn, w_in, layer), ffn_out_bf16[2 * layer + which]

    xn = _rms_cast(x, g4)
    conv_state = band_k = band_v = None
    for i in range(depth):
        m = i // 2
        a, w_out = ffn_in(xn, w_ffn1_in, w_ffn1_out, i, 0)
        x, xn = _mm_res_norm(a, w_out, 0, x, g4, (i, 1), (i, 2), 0.5, TM_FFN_OUT)
        if i % 2 == 0:
            if prompt:
                z, conv_state = _conv_in_prompt(xn, w_conv_in, w_conv_dw, m, batch, seq)
            else:
                z, conv_state = _conv_in_sample(xn, w_conv_in, w_conv_dw, m, conv_hist, seq)
            x, xn = _mm_res_norm(z, w_conv_out, m, x, g4, (i, 3), (i, 4), 1.0, TM_MIX_OUT)
        else:
            if prompt:
                qkv_hm, band_k, band_v = _qkv(xn, w_attn_qkv, m, seq // TM_QKV)
                o = _band_attn_prompt(qkv_hm, bias_tables[0], batch, seq)
            else:
                qkv_hm, band_k, band_v = _qkv(xn, w_attn_qkv, m, 1)
                o = _band_attn_sample(qkv_hm, band_cache[0], band_cache[1], bias_tables[1], bias_tables[2],
                                      batch, seq)
            x, xn = _mm_res_norm(o, w_attn_o, m, x, g4, (i, 3), (i, 4), 1.0, TM_MIX_OUT)
        x, xn = _mem_attn(xn, x, mem_k, mem_v, w_mem_q, w_mem_o, i, g4, batch, tm_mem)
        a, w_out = ffn_in(xn, w_ffn2_in, w_ffn2_out, i, 1)
        last = i == depth - 1
        x, xn = _mm_res_norm(a, w_out, 0, x, g4, (i, 7), ((i + 1) % depth, 0), 0.5, TM_FFN_OUT,
                             emit_xn=not last)
    return x, conv_state, band_k, band_v, ffn_out_bf16


def kernel(x_prompt, x_sample, state_conv, cache_band_k, cache_band_v, cache_mem_k, cache_mem_v, mem_prompt, g_norm, g_mem, w_ffn1_in, w_ffn1_out, w_ffn2_in, w_ffn2_out, w_conv_in, w_conv_dw, w_conv_out, w_attn_qkv, rel_bias, w_attn_o, w_mem_q, w_mem_kv, w_mem_o):
    batch, seq, d = x_prompt.shape
    dec_batch, dec_seq, _ = x_sample.shape
    depth = g_norm.shape[0]
    assert depth == 2 and state_conv.shape[0] == 1 and cache_band_k.shape[0] == 1
    assert cache_band_k.shape[2] == PAST_WIN and dec_seq == CHUNK and seq % QUERY_GROUP == 0
    assert seq % TM_IN == 0 and seq >= PAST_WIN and TM_IN % dec_seq == 0 and TM_QKV % dec_seq == 0

    weights = (
        g_norm[:, :, None, :],
        w_ffn1_in, w_ffn1_out, w_ffn2_in, w_ffn2_out,
        w_conv_in, w_conv_dw, w_conv_out.astype(BF16),
        w_attn_qkv.astype(BF16), w_attn_o.astype(BF16), w_mem_q.astype(BF16), w_mem_o.astype(BF16),
    )
    table = _band_bias_table(rel_bias[0])
    bias_tables = (table, table[:, :CHUNK, :PAST_WIN], table[:, :CHUNK, PAST_WIN:BAND])

    mem_rows = N_MEM * MEM_HEADS
    mem_k_p, mem_v_p = _mem_project(mem_prompt.reshape(batch * N_MEM, d), g_mem[:, None, :], w_mem_kv, 512)

    y_p, conv_p, bk_p, bv_p, ffn_out_bf16 = _trunk(
        x_prompt.reshape(batch * seq, d),
        mem_k_p.reshape(depth, batch, mem_rows, MEM_HEAD_DIM),
        mem_v_p.reshape(depth, batch, mem_rows, MEM_HEAD_DIM), weights,
        batch=batch, seq=seq, conv_hist=None, band_cache=None, bias_tables=bias_tables)

    y_s, conv_s, bk_s, bv_s, _ = _trunk(
        x_sample.reshape(dec_batch * dec_seq, d),
        cache_mem_k.reshape(depth, dec_batch, mem_rows, MEM_HEAD_DIM),
        cache_mem_v.reshape(depth, dec_batch, mem_rows, MEM_HEAD_DIM), weights,
        batch=dec_batch, seq=dec_seq, conv_hist=state_conv[0],
        band_cache=(cache_band_k[0], cache_band_v[0]),
        bias_tables=bias_tables, ffn_out_bf16=ffn_out_bf16)

    mem_shape = (depth, batch, N_MEM, MEM_HEADS, MEM_HEAD_DIM)
    band_p_shape = (1, batch, PAST_WIN, N_HEADS, HEAD_DIM)
    band_s_shape = (1, dec_batch, dec_seq, N_HEADS, HEAD_DIM)
    return (y_p.reshape(batch, seq, d), y_s.reshape(dec_batch, dec_seq, d),
            conv_p[None], bk_p.reshape(band_p_shape), bv_p.reshape(band_p_shape),
            mem_k_p.reshape(mem_shape), mem_v_p.reshape(mem_shape),
            conv_s[None], bk_s.reshape(band_s_shape), bv_s.reshape(band_s_shape))
```

```python
import functools

import jax
import jax.numpy as jnp
from jax import lax
from jax.experimental import pallas as pl
from jax.experimental.pallas import tpu as pltpu

F32 = jnp.float32
BF16 = jnp.bfloat16

D_MODEL = 2048
D_FF = 5632
N_HEADS = 16
HEAD_DIM = 128
CHUNK = 64
PAST_WIN = 512
BAND = PAST_WIN + CHUNK
MAX_REL = 128
N_MEM = 256
MEM_HEADS = 4
MEM_HEAD_DIM = 128
MEM_WIDTH = MEM_HEADS * MEM_HEAD_DIM
EPS = 1e-6
NEG_INF = -1e30
LOG2_E = 1.4426950408889634

V7X_VMEM_LIMIT_BYTES = 58 * 1024 * 1024
BF16_SUBLANES = 16
LANES = 128
EPILOGUE_PIECES = 4
FF_TILE = 512
CONV_TILE = 512
TM_QKV = PAST_WIN
QUERY_GROUP = 4 * CHUNK
HEADS_PER_STEP = 4
TM_IN = 1024
TM_FFN_IN = 2048
TM_FFN_OUT = 256
TM_MIX_OUT = 512
TM_MEM = 512
TM_MEM_SAMPLE = 256


def _params(sem):
    return pltpu.CompilerParams(dimension_semantics=sem, vmem_limit_bytes=V7X_VMEM_LIMIT_BYTES)


def _layer_spec(shape, layer):
    zeros = (0,) * len(shape)
    return pl.BlockSpec((None,) + tuple(shape), lambda *_: (layer,) + zeros, pipeline_mode=pl.Buffered(1))


def _gain_spec(layer, n):
    return pl.BlockSpec((None, None, 1, D_MODEL), lambda *_: (layer, n, 0, 0), pipeline_mode=pl.Buffered(1))


def _rms(x, g):
    return x * lax.rsqrt(jnp.mean(x * x, axis=-1, keepdims=True) + EPS) * g


def _rms_cast_kernel(x_ref, g_ref, o_ref):
    o_ref[...] = _rms(x_ref[...], g_ref[...]).astype(BF16)


def _rms_cast(x, g4):
    t, d = x.shape
    return pl.pallas_call(
        _rms_cast_kernel,
        out_shape=jax.ShapeDtypeStruct((t, d), BF16),
        grid=(t // TM_IN,),
        in_specs=[pl.BlockSpec((TM_IN, d), lambda i: (i, 0)), _gain_spec(0, 0)],
        out_specs=pl.BlockSpec((TM_IN, d), lambda i: (i, 0)),
        compiler_params=_params(("arbitrary",)),
        name="rms_cast",
    )(x, g4)


def _res_norm_epilogue(acc_ref, x_ref, gp, gn, xo_ref, xno_ref, row0, n_rows):
    rows = BF16_SUBLANES
    xnew = None
    for r in range(row0, row0 + n_rows, rows):
        sl = slice(r, r + rows)
        xnew = x_ref[sl, :] + _rms(acc_ref[sl, :], gp)
        xo_ref[sl, :] = xnew
        if xno_ref is not None:
            xno_ref[sl, :] = _rms(xnew, gn).astype(BF16)
    return xnew[:, :LANES]


def _zero_from(x):
    sixteen = jnp.uint32(16)
    bits = lax.bitcast_convert_type(x, jnp.uint32)
    return lax.shift_right_logical(lax.shift_right_logical(bits, sixteen), sixteen).astype(F32)


def _after(lhs, zero):
    r = BF16_SUBLANES
    first = (lhs[:r, :LANES].astype(F32) + zero).astype(lhs.dtype)
    top = jnp.concatenate([first, lhs[:r, LANES:]], axis=1)
    return jnp.concatenate([top, lhs[r:, :]], axis=0)


def _skewed_step(load_lhs, w_ref, x_ref, gp_ref, gn_ref, xo_ref, xno_ref, acc0_ref, acc1_ref, scale):
    i = pl.program_id(0)
    tm, d = acc0_ref.shape
    rows = tm // EPILOGUE_PIECES
    cols = d // EPILOGUE_PIECES

    @pl.when(i == 0)
    def _():
        acc1_ref[...] = jnp.zeros_like(acc1_ref)

    def step(prev_ref, cur_ref):
        gp = gp_ref[...] * scale
        gn = gn_ref[...]
        lhs = load_lhs()
        for k in range(EPILOGUE_PIECES):
            c = slice(k * cols, (k + 1) * cols)
            cur_ref[:, c] = jnp.dot(lhs, w_ref[:, c], preferred_element_type=F32)
            done = _res_norm_epilogue(prev_ref, x_ref, gp, gn, xo_ref, xno_ref, k * rows, rows)
            lhs = _after(lhs, _zero_from(done))

    @pl.when(i % 2 == 0)
    def _():
        step(acc1_ref, acc0_ref)

    @pl.when(i % 2 == 1)
    def _():
        step(acc0_ref, acc1_ref)


def _load_rows(a_ref):
    if len(a_ref.shape) == 2:
        return a_ref[...]
    return jnp.concatenate([a_ref[h] for h in range(a_ref.shape[0])], axis=1)


def _mm_res_norm_kernel(a_ref, w_ref, x_ref, gp_ref, gn_ref, xo_ref, *rest, scale, emit_xn):
    xno_ref, acc0_ref, acc1_ref = rest if emit_xn else (None,) + rest
    _skewed_step(lambda: _load_rows(a_ref), w_ref, x_ref, gp_ref, gn_ref, xo_ref, xno_ref,
                 acc0_ref, acc1_ref, scale)


def _mm_res_norm(a, w, w_layer, x, g4, post_gain, next_gain, scale, tm, emit_xn=True):
    t, d = x.shape
    k = w.shape[1]
    n = t // tm
    cur = lambda i: jnp.minimum(i, n - 1)
    prev = lambda i: (jnp.maximum(i - 1, 0), 0)
    if a.ndim == 2:
        a_spec = pl.BlockSpec((tm, k), lambda i: (cur(i), 0))
    else:
        a_spec = pl.BlockSpec((a.shape[0], tm, a.shape[2]), lambda i: (0, cur(i), 0))
    out_shape = [jax.ShapeDtypeStruct((t, d), F32)]
    out_specs = [pl.BlockSpec((tm, d), prev)]
    if emit_xn:
        out_shape.append(jax.ShapeDtypeStruct((t, d), BF16))
        out_specs.append(pl.BlockSpec((tm, d), prev))
    outs = pl.pallas_call(
        functools.partial(_mm_res_norm_kernel, scale=scale, emit_xn=emit_xn),
        out_shape=out_shape,
        grid=(n + 1,),
        in_specs=[
            a_spec,
            _layer_spec((k, d), w_layer),
            pl.BlockSpec((tm, d), prev),
            _gain_spec(*post_gain),
            _gain_spec(*next_gain),
        ],
        out_specs=out_specs,
        scratch_shapes=[pltpu.VMEM((tm, d), F32), pltpu.VMEM((tm, d), F32)],
        compiler_params=_params(("arbitrary",)),
        name="mm_res_norm",
    )(a, w, x, g4, g4)
    return (outs[0], outs[1]) if emit_xn else (outs[0], None)


def _ffn_in_kernel(xn_ref, wg_ref, wu_ref, *rest, cast_w_out):
    if cast_w_out:
        wo_f32_ref, o_ref, wo_bf16_ref, w_ref = rest
        wo_bf16_ref[...] = wo_f32_ref[...].astype(BF16)
    else:
        o_ref, w_ref = rest

    @pl.when(pl.program_id(1) == 0)
    def _():
        w_ref[:, :FF_TILE] = wg_ref[...].astype(BF16)
        w_ref[:, FF_TILE:] = wu_ref[...].astype(BF16)

    half = xn_ref.shape[0] // 2
    for r in (slice(0, half), slice(half, 2 * half)):
        h = jnp.dot(xn_ref[r, :], w_ref[...], preferred_element_type=F32)
        o_ref[r, :] = (jax.nn.silu(h[:, :FF_TILE]) * h[:, FF_TILE:]).astype(BF16)


def _ffn_in(xn, w_in, layer, w_out=None):
    t, d = xn.shape
    nj = D_FF // FF_TILE
    ni = t // TM_FFN_IN
    in_specs = [
        pl.BlockSpec((TM_FFN_IN, d), lambda j, i: (i, 0)),
        pl.BlockSpec((None, d, FF_TILE), lambda j, i: (layer, 0, j)),
        pl.BlockSpec((None, d, FF_TILE), lambda j, i: (layer, 0, nj + j)),
    ]
    out_shape = [jax.ShapeDtypeStruct((t, D_FF), BF16)]
    out_specs = [pl.BlockSpec((TM_FFN_IN, FF_TILE), lambda j, i: (i, j))]
    operands = [xn, w_in, w_in]
    if w_out is not None:
        rows = D_FF // (nj * ni)
        assert rows * nj * ni == D_FF and rows % BF16_SUBLANES == 0
        in_specs.append(pl.BlockSpec((None, rows, d), lambda j, i: (layer, j * ni + i, 0)))
        out_shape.append(jax.ShapeDtypeStruct((1, D_FF, d), BF16))
        out_specs.append(pl.BlockSpec((None, rows, d), lambda j, i: (0, j * ni + i, 0)))
        operands.append(w_out)
    outs = pl.pallas_call(
        functools.partial(_ffn_in_kernel, cast_w_out=w_out is not None),
        out_shape=out_shape,
        grid=(nj, ni),
        in_specs=in_specs,
        out_specs=out_specs,
        scratch_shapes=[pltpu.VMEM((d, 2 * FF_TILE), BF16)],
        compiler_params=_params(("arbitrary", "arbitrary")),
        name="ffn_in",
    )(*operands)
    return outs if w_out is not None else outs[0]


def _qkv_kernel(xn_ref, w_ref, hm_ref, k_ref, v_ref, *, keep_every):
    j, i = pl.program_id(0), pl.program_id(1)
    acc = jnp.dot(xn_ref[...], w_ref[...], preferred_element_type=F32)
    heads = [acc[:, h * HEAD_DIM:(h + 1) * HEAD_DIM] for h in range(N_HEADS)]
    hm_scale = jnp.where(j == 0, HEAD_DIM ** -0.5 * LOG2_E, 1.0)
    for h in range(N_HEADS):
        hm_ref[h] = (heads[h] * hm_scale).astype(BF16)

    def keep(ref):
        for h in range(N_HEADS):
            ref[pl.ds(h, acc.shape[0], stride=N_HEADS), :] = heads[h]

    kept_tile = i % keep_every == keep_every - 1

    @pl.when((j == 1) & kept_tile)
    def _():
        keep(k_ref)

    @pl.when((j == 2) & kept_tile)
    def _():
        keep(v_ref)


def _qkv(xn, w_qkv, layer, keep_every):
    t, d = xn.shape
    n_keep = t // (TM_QKV * keep_every)
    kept = jax.ShapeDtypeStruct((n_keep, TM_QKV * N_HEADS, HEAD_DIM), F32)

    def k_index(j, i):
        return (jnp.where(j < 1, 0, jnp.where(j > 1, n_keep - 1, i // keep_every)), 0, 0)

    def v_index(j, i):
        return (jnp.where(j < 2, 0, i // keep_every), 0, 0)

    return pl.pallas_call(
        functools.partial(_qkv_kernel, keep_every=keep_every),
        out_shape=[jax.ShapeDtypeStruct((3 * N_HEADS, t, HEAD_DIM), BF16), kept, kept],
        grid=(3, t // TM_QKV),
        in_specs=[
            pl.BlockSpec((TM_QKV, d), lambda j, i: (i, 0)),
            pl.BlockSpec((None, d, D_MODEL), lambda j, i: (layer, 0, j), pipeline_mode=pl.Buffered(1)),
        ],
        out_specs=[
            pl.BlockSpec((N_HEADS, TM_QKV, HEAD_DIM), lambda j, i: (j, i, 0)),
            pl.BlockSpec((None, TM_QKV * N_HEADS, HEAD_DIM), k_index),
            pl.BlockSpec((None, TM_QKV * N_HEADS, HEAD_DIM), v_index),
        ],
        compiler_params=_params(("arbitrary", "arbitrary")),
        name="qkv",
    )(xn, w_qkv)


def _cast_conv_weights(wb_ref, wc_ref, wx_ref, w_ref):
    @pl.when(pl.program_id(1) == 0)
    def _():
        for c in range(CONV_TILE // LANES):
            lane = slice(c * LANES, (c + 1) * LANES)
            for part, ref in enumerate((wb_ref, wc_ref, wx_ref)):
                w_ref[:, (3 * c + part) * LANES:(3 * c + part + 1) * LANES] = ref[:, lane].astype(BF16)


def _gated_conv(h, hist0, hist1, wdw, seq):
    tm = h.shape[0]
    nseq = tm // seq
    row = lax.broadcasted_iota(jnp.int32, (nseq, seq, LANES), 1)
    zs, us = [], []
    for c in range(CONV_TILE // LANES):
        lane = slice(c * LANES, (c + 1) * LANES)
        bg, cg, xv = (h[:, (3 * c + part) * LANES:(3 * c + part + 1) * LANES] for part in range(3))
        u = cg * xv
        u3 = u.reshape(nseq, seq, LANES)
        prev1 = pltpu.roll(u, 1, axis=0).reshape(u3.shape)
        prev2 = pltpu.roll(u, 2, axis=0).reshape(u3.shape)
        prev1 = jnp.where(row == 0, hist1[:, :, lane], prev1)
        prev2 = jnp.where(row == 0, hist0[:, :, lane], jnp.where(row == 1, hist1[:, :, lane], prev2))
        y = wdw[0:1, lane] * prev2 + wdw[1:2, lane] * prev1 + wdw[2:3, lane] * u3
        zs.append((bg.reshape(u3.shape) * y).astype(BF16).reshape(tm, LANES))
        us.append(u3)
    return jnp.concatenate(zs, axis=1), jnp.concatenate(us, axis=2)


def _conv_in_prompt_kernel(xn_ref, wb_ref, wc_ref, wx_ref, wdw_ref, z_ref, st_ref, w_ref, carry_ref,
                           *, tiles_per_seq):
    tm = xn_ref.shape[0]
    _cast_conv_weights(wb_ref, wc_ref, wx_ref, w_ref)

    @pl.when(pl.program_id(1) % tiles_per_seq == 0)
    def _():
        carry_ref[...] = jnp.zeros_like(carry_ref)

    h = jnp.dot(xn_ref[...], w_ref[...], preferred_element_type=F32)
    hist0 = carry_ref[0:1, :].reshape(1, 1, CONV_TILE)
    hist1 = carry_ref[1:2, :].reshape(1, 1, CONV_TILE)
    z, u3 = _gated_conv(h, hist0, hist1, wdw_ref[...], tm)
    z_ref[...] = z
    last = u3[0, tm - 2:, :]
    carry_ref[0:2, :] = last
    st_ref[0] = last


def _conv_weight_specs(d, layer):
    nj = d // CONV_TILE
    part = lambda p: pl.BlockSpec((None, d, CONV_TILE), lambda j, i: (layer, 0, p * nj + j),
                                  pipeline_mode=pl.Buffered(1))
    return [part(0), part(1), part(2), pl.BlockSpec((None, 3, CONV_TILE), lambda j, i: (layer, 0, j))]


def _conv_in_prompt(xn, w_in, w_dw, layer, batch, seq):
    t, d = xn.shape
    tiles_per_seq = seq // TM_IN
    return pl.pallas_call(
        functools.partial(_conv_in_prompt_kernel, tiles_per_seq=tiles_per_seq),
        out_shape=[jax.ShapeDtypeStruct((t, d), BF16), jax.ShapeDtypeStruct((batch, 2, d), F32)],
        grid=(d // CONV_TILE, t // TM_IN),
        in_specs=[pl.BlockSpec((TM_IN, d), lambda j, i: (i, 0))] + _conv_weight_specs(d, layer),
        out_specs=[
            pl.BlockSpec((TM_IN, CONV_TILE), lambda j, i: (i, j)),
            pl.BlockSpec((1, 2, CONV_TILE), lambda j, i: (i // tiles_per_seq, 0, j)),
        ],
        scratch_shapes=[pltpu.VMEM((d, 3 * CONV_TILE), BF16), pltpu.VMEM((8, CONV_TILE), F32)],
        compiler_params=_params(("arbitrary", "arbitrary")),
        name="conv_in_prompt",
    )(xn, w_in, w_in, w_in, w_dw)


def _conv_in_sample_kernel(xn_ref, wb_ref, wc_ref, wx_ref, wdw_ref, hist_ref, z_ref, st_ref, w_ref, *, seq):
    _cast_conv_weights(wb_ref, wc_ref, wx_ref, w_ref)
    h = jnp.dot(xn_ref[...], w_ref[...], preferred_element_type=F32)
    hist = hist_ref[...]
    z, u3 = _gated_conv(h, hist[:, 0:1, :], hist[:, 1:2, :], wdw_ref[...], seq)
    z_ref[...] = z
    st_ref[...] = u3[:, seq - 2:, :]


def _conv_in_sample(xn, w_in, w_dw, layer, hist, seq):
    t, d = xn.shape
    nseq = TM_IN // seq
    hist_spec = pl.BlockSpec((nseq, 2, CONV_TILE), lambda j, i: (i, 0, j))
    return pl.pallas_call(
        functools.partial(_conv_in_sample_kernel, seq=seq),
        out_shape=[jax.ShapeDtypeStruct((t, d), BF16), jax.ShapeDtypeStruct(hist.shape, F32)],
        grid=(d // CONV_TILE, t // TM_IN),
        in_specs=[pl.BlockSpec((TM_IN, d), lambda j, i: (i, 0))] + _conv_weight_specs(d, layer) + [hist_spec],
        out_specs=[pl.BlockSpec((TM_IN, CONV_TILE), lambda j, i: (i, j)), hist_spec],
        scratch_shapes=[pltpu.VMEM((d, 3 * CONV_TILE), BF16)],
        compiler_params=_params(("arbitrary", "arbitrary")),
        name="conv_in_sample",
    )(xn, w_in, w_in, w_in, w_dw, hist)


def _qk(q, k):
    return lax.dot_general(q, k, (((1,), (1,)), ((), ())), preferred_element_type=F32)


def _band_attn_prompt_kernel(q_ref, k_ref, v_ref, bias_ref, o_ref):
    seq = q_ref.shape[1]
    items = [(g, m) for g in range(q_ref.shape[0]) for m in range(seq // QUERY_GROUP)]

    def key_range(m):
        q0 = m * QUERY_GROUP
        return max(0, q0 - PAST_WIN), q0 + QUERY_GROUP

    def scores(g, m):
        q0 = m * QUERY_GROUP
        lo, hi = key_range(m)
        col0 = lo - (q0 - PAST_WIN)
        return _qk(q_ref[g, q0:hi, :], k_ref[g, lo:hi, :]) + bias_ref[g, :, col0:col0 + hi - lo]

    s_next = scores(*items[0])
    for n, (g, m) in enumerate(items):
        s = s_next
        if n + 1 < len(items):
            s_next = scores(*items[n + 1])
        lo, hi = key_range(m)
        q0 = m * QUERY_GROUP
        col0 = lo - (q0 - PAST_WIN)
        half = QUERY_GROUP // 2
        for r0 in (0, half):
            c_lo = max(r0 - col0, 0)
            c_hi = min(r0 + PAST_WIN + half, PAST_WIN + QUERY_GROUP) - col0
            sp = s[r0:r0 + half, c_lo:c_hi]
            p = jnp.exp2(sp - jnp.max(sp, axis=-1, keepdims=True))
            l = jnp.sum(p, axis=-1, keepdims=True)
            o = jnp.dot(p.astype(BF16), v_ref[g, lo + c_lo:lo + c_hi, :], preferred_element_type=F32) / l
            o_ref[g, q0 + r0:q0 + r0 + half, :] = o.astype(BF16)


def _band_attn_prompt(qkv_hm, bias, batch, seq):
    hps = HEADS_PER_STEP
    per_part = N_HEADS // hps
    head = lambda part: pl.BlockSpec((hps, seq, HEAD_DIM), lambda h, b: (part * per_part + h, b, 0))
    return pl.pallas_call(
        _band_attn_prompt_kernel,
        out_shape=jax.ShapeDtypeStruct((N_HEADS, batch * seq, HEAD_DIM), BF16),
        grid=(per_part, batch),
        in_specs=[head(0), head(1), head(2),
                  pl.BlockSpec((hps,) + bias.shape[1:], lambda h, b: (h, 0, 0))],
        out_specs=head(0),
        compiler_params=_params(("arbitrary", "arbitrary")),
        name="band_attn_prompt",
    )(qkv_hm, qkv_hm, qkv_hm, bias)


def _band_attn_sample_kernel(qkv_ref, ck_ref, cv_ref, bias_c_ref, bias_n_ref, o_ref):
    k_heads = pltpu.einshape("thd->htd", ck_ref[...]).astype(BF16)
    v_heads = pltpu.einshape("thd->htd", cv_ref[...]).astype(BF16)
    for h in range(N_HEADS):
        q, kn, vn = qkv_ref[h], qkv_ref[N_HEADS + h], qkv_ref[2 * N_HEADS + h]
        kc = k_heads[h]
        vc = v_heads[h]
        s_c = _qk(q, kc) + bias_c_ref[h]
        s_n = _qk(q, kn) + bias_n_ref[h]
        mx = jnp.maximum(jnp.max(s_c, axis=-1, keepdims=True), jnp.max(s_n, axis=-1, keepdims=True))
        p_c = jnp.exp2(s_c - mx)
        p_n = jnp.exp2(s_n - mx)
        l = jnp.sum(p_c, axis=-1, keepdims=True) + jnp.sum(p_n, axis=-1, keepdims=True)
        o = (jnp.dot(p_c.astype(BF16), vc, preferred_element_type=F32)
             + jnp.dot(p_n.astype(BF16), vn, preferred_element_type=F32)) / l
        o_ref[h] = o.astype(BF16)


def _band_attn_sample(qkv_hm, ck, cv, bias_c, bias_n, batch, seq):
    cache = pl.BlockSpec((None, PAST_WIN, N_HEADS, HEAD_DIM), lambda b: (b, 0, 0, 0))
    const = lambda a: pl.BlockSpec(a.shape, lambda b: (0,) * a.ndim, pipeline_mode=pl.Buffered(1))
    return pl.pallas_call(
        _band_attn_sample_kernel,
        out_shape=jax.ShapeDtypeStruct((N_HEADS, batch * seq, HEAD_DIM), BF16),
        grid=(batch,),
        in_specs=[
            pl.BlockSpec((3 * N_HEADS, seq, HEAD_DIM), lambda b: (0, b, 0)),
            cache, cache,
            const(bias_c), const(bias_n),
        ],
        out_specs=pl.BlockSpec((N_HEADS, seq, HEAD_DIM), lambda b: (0, b, 0)),
        compiler_params=_params(("arbitrary",)),
        name="band_attn_sample",
    )(qkv_hm, ck, cv, bias_c, bias_n)


def _band_bias_table(rel_bias):
    nh = rel_bias.shape[0]
    n_keys = PAST_WIN + QUERY_GROUP
    period = n_keys + QUERY_GROUP
    n_far_past = PAST_WIN - MAX_REL + QUERY_GROUP - 1
    n_future = period - n_far_past - rel_bias.shape[1]
    f = jnp.concatenate([
        jnp.broadcast_to(rel_bias[:, -1:], (nh, n_far_past)),
        rel_bias[:, ::-1],
        jnp.broadcast_to(rel_bias[:, :1], (nh, n_future)),
    ], axis=1)
    f = jnp.roll(f, -(QUERY_GROUP - 1), axis=1)
    flat = jnp.tile(f, (1, QUERY_GROUP))[:, :QUERY_GROUP * (period - 1)]
    toeplitz = flat.reshape(nh, QUERY_GROUP, period - 1)[:, :, :n_keys]
    qi = jnp.arange(QUERY_GROUP)[:, None]
    kj = jnp.arange(n_keys)[None, :]
    band_lo = (qi // CHUNK) * CHUNK
    in_band = (kj >= band_lo) & (kj < band_lo + BAND)
    return jnp.where(in_band[None], toeplitz * LOG2_E, NEG_INF).astype(F32)


def _mem_attn_kernel(xn_ref, wq_ref, mk_ref, mv_ref, wo_ref, x_ref, gp_ref, gn_ref,
                     xo_ref, xno_ref, acc0_ref, acc1_ref):
    def attend():
        q = jnp.dot(xn_ref[...], wq_ref[...], preferred_element_type=F32) * (MEM_HEAD_DIM ** -0.5 * LOG2_E)
        n_streams = mk_ref.shape[0]
        rows_per_stream = q.shape[0] // n_streams
        streams = []
        for b in range(n_streams):
            rows = slice(b * rows_per_stream, (b + 1) * rows_per_stream)
            heads = []
            for h in range(MEM_HEADS):
                c = slice(h * MEM_HEAD_DIM, (h + 1) * MEM_HEAD_DIM)
                head_rows = pl.ds(h, N_MEM, stride=MEM_HEADS)
                s = _qk(q[rows, c].astype(BF16), mk_ref[b, head_rows, :].astype(BF16))
                p = jnp.exp2(s - jnp.max(s, axis=-1, keepdims=True))
                l = jnp.sum(p, axis=-1, keepdims=True)
                o = jnp.dot(p.astype(BF16), mv_ref[b, head_rows, :].astype(BF16), preferred_element_type=F32) / l
                heads.append(o.astype(BF16))
            streams.append(jnp.concatenate(heads, axis=1))
        return streams[0] if n_streams == 1 else jnp.concatenate(streams, axis=0)

    _skewed_step(attend, wo_ref, x_ref, gp_ref, gn_ref, xo_ref, xno_ref, acc0_ref, acc1_ref, 1.0)


def _mem_attn(xn, x, mk, mv, wq, wo, layer, g4, batch, tm):
    t, d = x.shape
    n = t // tm
    seq = t // batch
    streams_per_tile = max(1, tm // seq)
    assert tm * n == t and (seq % tm == 0 or tm % seq == 0)
    cur = lambda i: (jnp.minimum(i, n - 1), 0)
    prev = lambda i: (jnp.maximum(i - 1, 0), 0)
    mem_spec = pl.BlockSpec(
        (None, streams_per_tile, N_MEM * MEM_HEADS, MEM_HEAD_DIM),
        lambda i: (layer, (jnp.minimum(i, n - 1) * tm) // (streams_per_tile * seq), 0, 0))
    return pl.pallas_call(
        _mem_attn_kernel,
        out_shape=[jax.ShapeDtypeStruct((t, d), F32), jax.ShapeDtypeStruct((t, d), BF16)],
        grid=(n + 1,),
        in_specs=[
            pl.BlockSpec((tm, d), cur),
            _layer_spec((d, MEM_WIDTH), layer),
            mem_spec, mem_spec,
            _layer_spec((MEM_WIDTH, d), layer),
            pl.BlockSpec((tm, d), prev),
            _gain_spec(layer, 5),
            _gain_spec(layer, 6),
        ],
        out_specs=[pl.BlockSpec((tm, d), prev), pl.BlockSpec((tm, d), prev)],
        scratch_shapes=[pltpu.VMEM((tm, d), F32), pltpu.VMEM((tm, d), F32)],
        compiler_params=_params(("arbitrary",)),
        name="mem_attn",
    )(xn, wq, mk, mv, wo, x, g4, g4)


def _mem_project_kernel(mem_ref, g_ref, wf_ref, k_ref, v_ref, w_ref):
    @pl.when(pl.program_id(1) == 0)
    def _():
        w_ref[...] = wf_ref[...].astype(BF16)

    kv = jnp.dot(_rms(mem_ref[...], g_ref[...]).astype(BF16), w_ref[...], preferred_element_type=F32)
    tm = kv.shape[0]
    for h in range(MEM_HEADS):
        head_rows = pl.ds(h, tm, stride=MEM_HEADS)
        k_ref[head_rows, :] = kv[:, h * MEM_HEAD_DIM:(h + 1) * MEM_HEAD_DIM]
        v_ref[head_rows, :] = kv[:, MEM_WIDTH + h * MEM_HEAD_DIM:MEM_WIDTH + (h + 1) * MEM_HEAD_DIM]


def _mem_project(mem, g_mem, w_kv, tm):
    t, d = mem.shape
    depth = w_kv.shape[0]
    out = jax.ShapeDtypeStruct((depth, t * MEM_HEADS, MEM_HEAD_DIM), F32)
    out_spec = pl.BlockSpec((None, tm * MEM_HEADS, MEM_HEAD_DIM), lambda l, i: (l, i, 0))
    return pl.pallas_call(
        _mem_project_kernel,
        out_shape=[out, out],
        grid=(depth, t // tm),
        in_specs=[
            pl.BlockSpec((tm, d), lambda l, i: (i, 0)),
            pl.BlockSpec((None, 1, d), lambda l, i: (l, 0, 0)),
            pl.BlockSpec((None, d, 2 * MEM_WIDTH), lambda l, i: (l, 0, 0)),
        ],
        out_specs=[out_spec, out_spec],
        scratch_shapes=[pltpu.VMEM((d, 2 * MEM_WIDTH), BF16)],
        compiler_params=_params(("arbitrary", "arbitrary")),
        name="mem_project",
    )(mem, g_mem, w_kv)


def _trunk(x, mem_k, mem_v, weights, *, batch, seq, conv_hist, band_cache, bias_tables, ffn_out_bf16=None):
    prompt = conv_hist is None
    (g4, w_ffn1_in, w_ffn1_out, w_ffn2_in, w_ffn2_out, w_conv_in, w_conv_dw, w_conv_out,
     w_attn_qkv, w_attn_o, w_mem_q, w_mem_o) = weights
    depth = g4.shape[0]
    tm_mem = TM_MEM if prompt else TM_MEM_SAMPLE
    cast_here = ffn_out_bf16 is None
    if cast_here:
        ffn_out_bf16 = []

    def ffn_in(xn, w_in, w_out, layer, which):
        if cast_here:
            a, w_out_bf16 = _ffn_in(xn, w_in, layer, w_out)
            ffn_out_bf16.append(w_out_bf16)
            return a, w_out_bf16
        return _ffn_in(xn, w_in, layer), ffn_out_bf16[2 * layer + which]

    xn = _rms_cast(x, g4)
    conv_state = band_k = band_v = None
    for i in range(depth):
        m = i // 2
        a, w_out = ffn_in(xn, w_ffn1_in, w_ffn1_out, i, 0)
        x, xn = _mm_res_norm(a, w_out, 0, x, g4, (i, 1), (i, 2), 0.5, TM_FFN_OUT)
        if i % 2 == 0:
            if prompt:
                z, conv_state = _conv_in_prompt(xn, w_conv_in, w_conv_dw, m, batch, seq)
            else:
                z, conv_state = _conv_in_sample(xn, w_conv_in, w_conv_dw, m, conv_hist, seq)
            x, xn = _mm_res_norm(z, w_conv_out, m, x, g4, (i, 3), (i, 4), 1.0, TM_MIX_OUT)
        else:
            if prompt:
                qkv_hm, band_k, band_v = _qkv(xn, w_attn_qkv, m, seq // TM_QKV)
                o = _band_attn_prompt(qkv_hm, bias_tables[0], batch, seq)
            else:
                qkv_hm, band_k, band_v = _qkv(xn, w_attn_qkv, m, 1)
                o = _band_attn_sample(qkv_hm, band_cache[0], band_cache[1], bias_tables[1], bias_tables[2],
                                      batch, seq)
            x, xn = _mm_res_norm(o, w_attn_o, m, x, g4, (i, 3), (i, 4), 1.0, TM_MIX_OUT)
        x, xn = _mem_attn(xn, x, mem_k, mem_v, w_mem_q, w_mem_o, i, g4, batch, tm_mem)
        a, w_out = ffn_in(xn, w_ffn2_in, w_ffn2_out, i, 1)
        last = i == depth - 1
        x, xn = _mm_res_norm(a, w_out, 0, x, g4, (i, 7), ((i + 1) % depth, 0), 0.5, TM_FFN_OUT,
                             emit_xn=not last)
    return x, conv_state, band_k, band_v, ffn_out_bf16


def kernel(x_prompt, x_sample, state_conv, cache_band_k, cache_band_v, cache_mem_k, cache_mem_v, mem_prompt, g_norm, g_mem, w_ffn1_in, w_ffn1_out, w_ffn2_in, w_ffn2_out, w_conv_in, w_conv_dw, w_conv_out, w_attn_qkv, rel_bias, w_attn_o, w_mem_q, w_mem_kv, w_mem_o):
    batch, seq, d = x_prompt.shape
    dec_batch, dec_seq, _ = x_sample.shape
    depth = g_norm.shape[0]
    assert depth == 2 and state_conv.shape[0] == 1 and cache_band_k.shape[0] == 1
    assert cache_band_k.shape[2] == PAST_WIN and dec_seq == CHUNK and seq % QUERY_GROUP == 0
    assert seq % TM_IN == 0 and seq >= PAST_WIN and TM_IN % dec_seq == 0 and TM_QKV % dec_seq == 0

    weights = (
        g_norm[:, :, None, :],
        w_ffn1_in, w_ffn1_out, w_ffn2_in, w_ffn2_out,
        w_conv_in, w_conv_dw, w_conv_out.astype(BF16),
        w_attn_qkv.astype(BF16), w_attn_o.astype(BF16), w_mem_q.astype(BF16), w_mem_o.astype(BF16),
    )
    table = _band_bias_table(rel_bias[0])
    bias_tables = (table, table[:, :CHUNK, :PAST_WIN], table[:, :CHUNK, PAST_WIN:BAND])

    mem_rows = N_MEM * MEM_HEADS
    mem_k_p, mem_v_p = _mem_project(mem_prompt.reshape(batch * N_MEM, d), g_mem[:, None, :], w_mem_kv, 512)

    y_p, conv_p, bk_p, bv_p, ffn_out_bf16 = _trunk(
        x_prompt.reshape(batch * seq, d),
        mem_k_p.reshape(depth, batch, mem_rows, MEM_HEAD_DIM),
        mem_v_p.reshape(depth, batch, mem_rows, MEM_HEAD_DIM), weights,
        batch=batch, seq=seq, conv_hist=None, band_cache=None, bias_tables=bias_tables)

    y_s, conv_s, bk_s, bv_s, _ = _trunk(
        x_sample.reshape(dec_batch * dec_seq, d),
        cache_mem_k.reshape(depth, dec_batch, mem_rows, MEM_HEAD_DIM),
        cache_mem_v.reshape(depth, dec_batch, mem_rows, MEM_HEAD_DIM), weights,
        batch=dec_batch, seq=dec_seq, conv_hist=state_conv[0],
        band_cache=(cache_band_k[0], cache_band_v[0]),
        bias_tables=bias_tables, ffn_out_bf16=ffn_out_bf16)

    mem_shape = (depth, batch, N_MEM, MEM_HEADS, MEM_HEAD_DIM)
    band_p_shape = (1, batch, PAST_WIN, N_HEADS, HEAD_DIM)
    band_s_shape = (1, dec_batch, dec_seq, N_HEADS, HEAD_DIM)
    return (y_p.reshape(batch, seq, d), y_s.reshape(dec_batch, dec_seq, d),
            conv_p[None], bk_p.reshape(band_p_shape), bv_p.reshape(band_p_shape),
            mem_k_p.reshape(mem_shape), mem_v_p.reshape(mem_shape),
            conv_s[None], bk_s.reshape(band_s_shape), bv_s.reshape(band_s_shape))
```

```python
import functools

import jax
import jax.numpy as jnp
from jax import lax
from jax.experimental import pallas as pl
from jax.experimental.pallas import tpu as pltpu

F32 = jnp.float32
BF16 = jnp.bfloat16

D_MODEL = 2048
D_FF = 5632
N_HEADS = 16
HEAD_DIM = 128
CHUNK = 64
PAST_WIN = 512
BAND = PAST_WIN + CHUNK
MAX_REL = 128
N_MEM = 256
MEM_HEADS = 4
MEM_HEAD_DIM = 128
MEM_WIDTH = MEM_HEADS * MEM_HEAD_DIM
EPS = 1e-6
NEG_INF = -1e30
LOG2_E = 1.4426950408889634

V7X_VMEM_LIMIT_BYTES = 58 * 1024 * 1024
BF16_SUBLANES = 16
LANES = 128
EPILOGUE_PIECES = 4
FF_TILE = 512
CONV_TILE = 512
TM_QKV = PAST_WIN
QUERY_GROUP = 4 * CHUNK
HEADS_PER_STEP = 4
TM_IN = 1024
TM_FFN_IN = 2048
TM_FFN_OUT = 256
TM_MIX_OUT = 512
TM_MEM = 512
TM_MEM_SAMPLE = 256


def _params(sem):
    return pltpu.CompilerParams(dimension_semantics=sem, vmem_limit_bytes=V7X_VMEM_LIMIT_BYTES)


def _layer_spec(shape, layer):
    zeros = (0,) * len(shape)
    return pl.BlockSpec((None,) + tuple(shape), lambda *_: (layer,) + zeros, pipeline_mode=pl.Buffered(1))


def _gain_spec(layer, n):
    return pl.BlockSpec((None, None, 1, D_MODEL), lambda *_: (layer, n, 0, 0), pipeline_mode=pl.Buffered(1))


def _rms(x, g):
    return x * lax.rsqrt(jnp.mean(x * x, axis=-1, keepdims=True) + EPS) * g


def _rms_cast_kernel(x_ref, g_ref, o_ref):
    o_ref[...] = _rms(x_ref[...], g_ref[...]).astype(BF16)


def _rms_cast(x, g4):
    t, d = x.shape
    return pl.pallas_call(
        _rms_cast_kernel,
        out_shape=jax.ShapeDtypeStruct((t, d), BF16),
        grid=(t // TM_IN,),
        in_specs=[pl.BlockSpec((TM_IN, d), lambda i: (i, 0)), _gain_spec(0, 0)],
        out_specs=pl.BlockSpec((TM_IN, d), lambda i: (i, 0)),
        compiler_params=_params(("arbitrary",)),
        name="rms_cast",
    )(x, g4)


def _res_norm_epilogue(acc_ref, x_ref, gp, gn, xo_ref, xno_ref, row0, n_rows):
    rows = BF16_SUBLANES
    xnew = None
    for r in range(row0, row0 + n_rows, rows):
        sl = slice(r, r + rows)
        xnew = x_ref[sl, :] + _rms(acc_ref[sl, :], gp)
        xo_ref[sl, :] = xnew
        if xno_ref is not None:
            xno_ref[sl, :] = _rms(xnew, gn).astype(BF16)
    return xnew[:, :LANES]


def _zero_from(x):
    sixteen = jnp.uint32(16)
    bits = lax.bitcast_convert_type(x, jnp.uint32)
    return lax.shift_right_logical(lax.shift_right_logical(bits, sixteen), sixteen).astype(F32)


def _after(lhs, zero):
    r = BF16_SUBLANES
    first = (lhs[:r, :LANES].astype(F32) + zero).astype(lhs.dtype)
    top = jnp.concatenate([first, lhs[:r, LANES:]], axis=1)
    return jnp.concatenate([top, lhs[r:, :]], axis=0)


def _skewed_step(load_lhs, w_ref, x_ref, gp_ref, gn_ref, xo_ref, xno_ref, acc0_ref, acc1_ref, scale, n_tiles):
    i = pl.program_id(0)
    tm, d = acc0_ref.shape
    rows = tm // EPILOGUE_PIECES
    cols = d // EPILOGUE_PIECES
    accs = (acc0_ref, acc1_ref)

    @pl.when(i == 0)
    def _():
        acc1_ref[...] = jnp.zeros_like(acc1_ref)

    def gains():
        return gp_ref[...] * scale, gn_ref[...]

    def step(prev_ref, cur_ref):
        gp, gn = gains()
        lhs = load_lhs()
        for k in range(EPILOGUE_PIECES):
            c = slice(k * cols, (k + 1) * cols)
            cur_ref[:, c] = jnp.dot(lhs, w_ref[:, c], preferred_element_type=F32)
            done = _res_norm_epilogue(prev_ref, x_ref, gp, gn, xo_ref, xno_ref, k * rows, rows)
            lhs = _after(lhs, _zero_from(done))

    for parity in (0, 1):
        @pl.when((i % 2 == parity) & (i < n_tiles))
        def _():
            step(accs[1 - parity], accs[parity])

    @pl.when(i == n_tiles)
    def _():
        gp, gn = gains()
        _res_norm_epilogue(accs[(n_tiles - 1) % 2], x_ref, gp, gn, xo_ref, xno_ref, 0, tm)


def _load_rows(a_ref):
    if len(a_ref.shape) == 2:
        return a_ref[...]
    return jnp.concatenate([a_ref[h] for h in range(a_ref.shape[0])], axis=1)


def _mm_res_norm_kernel(a_ref, w_ref, x_ref, gp_ref, gn_ref, xo_ref, *rest, scale, emit_xn, n_tiles):
    xno_ref, acc0_ref, acc1_ref = rest if emit_xn else (None,) + rest
    _skewed_step(lambda: _load_rows(a_ref), w_ref, x_ref, gp_ref, gn_ref, xo_ref, xno_ref,
                 acc0_ref, acc1_ref, scale, n_tiles)


def _mm_res_norm(a, w, w_layer, x, g4, post_gain, next_gain, scale, tm, emit_xn=True):
    t, d = x.shape
    k = w.shape[1]
    n = t // tm
    cur = lambda i: jnp.minimum(i, n - 1)
    prev = lambda i: (jnp.maximum(i - 1, 0), 0)
    if a.ndim == 2:
        a_spec = pl.BlockSpec((tm, k), lambda i: (cur(i), 0))
    else:
        a_spec = pl.BlockSpec((a.shape[0], tm, a.shape[2]), lambda i: (0, cur(i), 0))
    out_shape = [jax.ShapeDtypeStruct((t, d), F32)]
    out_specs = [pl.BlockSpec((tm, d), prev)]
    if emit_xn:
        out_shape.append(jax.ShapeDtypeStruct((t, d), BF16))
        out_specs.append(pl.BlockSpec((tm, d), prev))
    outs = pl.pallas_call(
        functools.partial(_mm_res_norm_kernel, scale=scale, emit_xn=emit_xn, n_tiles=n),
        out_shape=out_shape,
        grid=(n + 1,),
        in_specs=[
            a_spec,
            _layer_spec((k, d), w_layer),
            pl.BlockSpec((tm, d), prev),
            _gain_spec(*post_gain),
            _gain_spec(*next_gain),
        ],
        out_specs=out_specs,
        scratch_shapes=[pltpu.VMEM((tm, d), F32), pltpu.VMEM((tm, d), F32)],
        compiler_params=_params(("arbitrary",)),
        name="mm_res_norm",
    )(a, w, x, g4, g4)
    return (outs[0], outs[1]) if emit_xn else (outs[0], None)


def _ffn_in_kernel(xn_ref, wg_ref, wu_ref, *rest, cast_w_out):
    if cast_w_out:
        wo_f32_ref, o_ref, wo_bf16_ref, w_ref = rest
        wo_bf16_ref[...] = wo_f32_ref[...].astype(BF16)
    else:
        o_ref, w_ref = rest

    @pl.when(pl.program_id(1) == 0)
    def _():
        w_ref[:, :FF_TILE] = wg_ref[...].astype(BF16)
        w_ref[:, FF_TILE:] = wu_ref[...].astype(BF16)

    half = xn_ref.shape[0] // 2
    for r in (slice(0, half), slice(half, 2 * half)):
        h = jnp.dot(xn_ref[r, :], w_ref[...], preferred_element_type=F32)
        o_ref[r, :] = (jax.nn.silu(h[:, :FF_TILE]) * h[:, FF_TILE:]).astype(BF16)


def _ffn_in(xn, w_in, layer, w_out=None):
    t, d = xn.shape
    nj = D_FF // FF_TILE
    ni = t // TM_FFN_IN
    in_specs = [
        pl.BlockSpec((TM_FFN_IN, d), lambda j, i: (i, 0)),
        pl.BlockSpec((None, d, FF_TILE), lambda j, i: (layer, 0, j)),
        pl.BlockSpec((None, d, FF_TILE), lambda j, i: (layer, 0, nj + j)),
    ]
    out_shape = [jax.ShapeDtypeStruct((t, D_FF), BF16)]
    out_specs = [pl.BlockSpec((TM_FFN_IN, FF_TILE), lambda j, i: (i, j))]
    operands = [xn, w_in, w_in]
    if w_out is not None:
        rows = D_FF // (nj * ni)
        assert rows * nj * ni == D_FF and rows % BF16_SUBLANES == 0
        in_specs.append(pl.BlockSpec((None, rows, d), lambda j, i: (layer, j * ni + i, 0)))
        out_shape.append(jax.ShapeDtypeStruct((1, D_FF, d), BF16))
        out_specs.append(pl.BlockSpec((None, rows, d), lambda j, i: (0, j * ni + i, 0)))
        operands.append(w_out)
    outs = pl.pallas_call(
        functools.partial(_ffn_in_kernel, cast_w_out=w_out is not None),
        out_shape=out_shape,
        grid=(nj, ni),
        in_specs=in_specs,
        out_specs=out_specs,
        scratch_shapes=[pltpu.VMEM((d, 2 * FF_TILE), BF16)],
        compiler_params=_params(("arbitrary", "arbitrary")),
        name="ffn_in",
    )(*operands)
    return outs if w_out is not None else outs[0]


def _qkv_kernel(xn_ref, w_ref, hm_ref, k_ref, v_ref, *, keep_every):
    j, i = pl.program_id(0), pl.program_id(1)
    acc = jnp.dot(xn_ref[...], w_ref[...], preferred_element_type=F32)
    heads = [acc[:, h * HEAD_DIM:(h + 1) * HEAD_DIM] for h in range(N_HEADS)]
    hm_scale = jnp.where(j == 0, HEAD_DIM ** -0.5 * LOG2_E, 1.0)
    for h in range(N_HEADS):
        hm_ref[h] = (heads[h] * hm_scale).astype(BF16)

    def keep(ref):
        for h in range(N_HEADS):
            ref[pl.ds(h, acc.shape[0], stride=N_HEADS), :] = heads[h]

    kept_tile = i % keep_every == keep_every - 1

    @pl.when((j == 1) & kept_tile)
    def _():
        keep(k_ref)

    @pl.when((j == 2) & kept_tile)
    def _():
        keep(v_ref)


def _qkv(xn, w_qkv, layer, keep_every):
    t, d = xn.shape
    n_keep = t // (TM_QKV * keep_every)
    kept = jax.ShapeDtypeStruct((n_keep, TM_QKV * N_HEADS, HEAD_DIM), F32)

    def k_index(j, i):
        return (jnp.where(j < 1, 0, jnp.where(j > 1, n_keep - 1, i // keep_every)), 0, 0)

    def v_index(j, i):
        return (jnp.where(j < 2, 0, i // keep_every), 0, 0)

    return pl.pallas_call(
        functools.partial(_qkv_kernel, keep_every=keep_every),
        out_shape=[jax.ShapeDtypeStruct((3 * N_HEADS, t, HEAD_DIM), BF16), kept, kept],
        grid=(3, t // TM_QKV),
        in_specs=[
            pl.BlockSpec((TM_QKV, d), lambda j, i: (i, 0)),
            pl.BlockSpec((None, d, D_MODEL), lambda j, i: (layer, 0, j), pipeline_mode=pl.Buffered(1)),
        ],
        out_specs=[
            pl.BlockSpec((N_HEADS, TM_QKV, HEAD_DIM), lambda j, i: (j, i, 0)),
            pl.BlockSpec((None, TM_QKV * N_HEADS, HEAD_DIM), k_index),
            pl.BlockSpec((None, TM_QKV * N_HEADS, HEAD_DIM), v_index),
        ],
        compiler_params=_params(("arbitrary", "arbitrary")),
        name="qkv",
    )(xn, w_qkv)


def _cast_conv_weights(wb_ref, wc_ref, wx_ref, w_ref):
    @pl.when(pl.program_id(1) == 0)
    def _():
        for c in range(CONV_TILE // LANES):
            lane = slice(c * LANES, (c + 1) * LANES)
            for part, ref in enumerate((wb_ref, wc_ref, wx_ref)):
                w_ref[:, (3 * c + part) * LANES:(3 * c + part + 1) * LANES] = ref[:, lane].astype(BF16)


def _gated_conv(h, hist0, hist1, wdw, seq):
    tm = h.shape[0]
    nseq = tm // seq
    row = lax.broadcasted_iota(jnp.int32, (nseq, seq, LANES), 1)
    zs, us = [], []
    for c in range(CONV_TILE // LANES):
        lane = slice(c * LANES, (c + 1) * LANES)
        bg, cg, xv = (h[:, (3 * c + part) * LANES:(3 * c + part + 1) * LANES] for part in range(3))
        u = cg * xv
        u3 = u.reshape(nseq, seq, LANES)
        prev1 = pltpu.roll(u, 1, axis=0).reshape(u3.shape)
        prev2 = pltpu.roll(u, 2, axis=0).reshape(u3.shape)
        prev1 = jnp.where(row == 0, hist1[:, :, lane], prev1)
        prev2 = jnp.where(row == 0, hist0[:, :, lane], jnp.where(row == 1, hist1[:, :, lane], prev2))
        y = wdw[0:1, lane] * prev2 + wdw[1:2, lane] * prev1 + wdw[2:3, lane] * u3
        zs.append((bg.reshape(u3.shape) * y).astype(BF16).reshape(tm, LANES))
        us.append(u3)
    return jnp.concatenate(zs, axis=1), jnp.concatenate(us, axis=2)


def _conv_in_prompt_kernel(xn_ref, wb_ref, wc_ref, wx_ref, wdw_ref, z_ref, st_ref, w_ref, carry_ref,
                           *, tiles_per_seq):
    tm = xn_ref.shape[0]
    _cast_conv_weights(wb_ref, wc_ref, wx_ref, w_ref)

    @pl.when(pl.program_id(1) % tiles_per_seq == 0)
    def _():
        carry_ref[...] = jnp.zeros_like(carry_ref)

    h = jnp.dot(xn_ref[...], w_ref[...], preferred_element_type=F32)
    hist0 = carry_ref[0:1, :].reshape(1, 1, CONV_TILE)
    hist1 = carry_ref[1:2, :].reshape(1, 1, CONV_TILE)
    z, u3 = _gated_conv(h, hist0, hist1, wdw_ref[...], tm)
    z_ref[...] = z
    last = u3[0, tm - 2:, :]
    carry_ref[0:2, :] = last
    st_ref[0] = last


def _conv_weight_specs(d, layer):
    nj = d // CONV_TILE
    part = lambda p: pl.BlockSpec((None, d, CONV_TILE), lambda j, i: (layer, 0, p * nj + j),
                                  pipeline_mode=pl.Buffered(1))
    return [part(0), part(1), part(2), pl.BlockSpec((None, 3, CONV_TILE), lambda j, i: (layer, 0, j))]


def _conv_in_prompt(xn, w_in, w_dw, layer, batch, seq):
    t, d = xn.shape
    tiles_per_seq = seq // TM_IN
    return pl.pallas_call(
        functools.partial(_conv_in_prompt_kernel, tiles_per_seq=tiles_per_seq),
        out_shape=[jax.ShapeDtypeStruct((t, d), BF16), jax.ShapeDtypeStruct((batch, 2, d), F32)],
        grid=(d // CONV_TILE, t // TM_IN),
        in_specs=[pl.BlockSpec((TM_IN, d), lambda j, i: (i, 0))] + _conv_weight_specs(d, layer),
        out_specs=[
            pl.BlockSpec((TM_IN, CONV_TILE), lambda j, i: (i, j)),
            pl.BlockSpec((1, 2, CONV_TILE), lambda j, i: (i // tiles_per_seq, 0, j)),
        ],
        scratch_shapes=[pltpu.VMEM((d, 3 * CONV_TILE), BF16), pltpu.VMEM((8, CONV_TILE), F32)],
        compiler_params=_params(("arbitrary", "arbitrary")),
        name="conv_in_prompt",
    )(xn, w_in, w_in, w_in, w_dw)


def _conv_in_sample_kernel(xn_ref, wb_ref, wc_ref, wx_ref, wdw_ref, hist_ref, z_ref, st_ref, w_ref, *, seq):
    _cast_conv_weights(wb_ref, wc_ref, wx_ref, w_ref)
    h = jnp.dot(xn_ref[...], w_ref[...], preferred_element_type=F32)
    hist = hist_ref[...]
    z, u3 = _gated_conv(h, hist[:, 0:1, :], hist[:, 1:2, :], wdw_ref[...], seq)
    z_ref[...] = z
    st_ref[...] = u3[:, seq - 2:, :]


def _conv_in_sample(xn, w_in, w_dw, layer, hist, seq):
    t, d = xn.shape
    nseq = TM_IN // seq
    hist_spec = pl.BlockSpec((nseq, 2, CONV_TILE), lambda j, i: (i, 0, j))
    return pl.pallas_call(
        functools.partial(_conv_in_sample_kernel, seq=seq),
        out_shape=[jax.ShapeDtypeStruct((t, d), BF16), jax.ShapeDtypeStruct(hist.shape, F32)],
        grid=(d // CONV_TILE, t // TM_IN),
        in_specs=[pl.BlockSpec((TM_IN, d), lambda j, i: (i, 0))] + _conv_weight_specs(d, layer) + [hist_spec],
        out_specs=[pl.BlockSpec((TM_IN, CONV_TILE), lambda j, i: (i, j)), hist_spec],
        scratch_shapes=[pltpu.VMEM((d, 3 * CONV_TILE), BF16)],
        compiler_params=_params(("arbitrary", "arbitrary")),
        name="conv_in_sample",
    )(xn, w_in, w_in, w_in, w_dw, hist)


def _qk(q, k):
    return lax.dot_general(q, k, (((1,), (1,)), ((), ())), preferred_element_type=F32)


def _band_attn_prompt_kernel(q_ref, k_ref, v_ref, bias_ref, o_ref):
    seq = q_ref.shape[1]
    items = [(g, m) for g in range(q_ref.shape[0]) for m in range(seq // QUERY_GROUP)]

    def key_range(m):
        q0 = m * QUERY_GROUP
        return max(0, q0 - PAST_WIN), q0 + QUERY_GROUP

    def scores(g, m):
        q0 = m * QUERY_GROUP
        lo, hi = key_range(m)
        col0 = lo - (q0 - PAST_WIN)
        return _qk(q_ref[g, q0:hi, :], k_ref[g, lo:hi, :]) + bias_ref[g, :, col0:col0 + hi - lo]

    s_next = scores(*items[0])
    for n, (g, m) in enumerate(items):
        s = s_next
        if n + 1 < len(items):
            s_next = scores(*items[n + 1])
        lo, hi = key_range(m)
        q0 = m * QUERY_GROUP
        col0 = lo - (q0 - PAST_WIN)
        half = QUERY_GROUP // 2
        for r0 in (0, half):
            c_lo = max(r0 - col0, 0)
            c_hi = min(r0 + PAST_WIN + half, PAST_WIN + QUERY_GROUP) - col0
            sp = s[r0:r0 + half, c_lo:c_hi]
            p = jnp.exp2(sp - jnp.max(sp, axis=-1, keepdims=True))
            l = jnp.sum(p, axis=-1, keepdims=True)
            o = jnp.dot(p.astype(BF16), v_ref[g, lo + c_lo:lo + c_hi, :], preferred_element_type=F32) / l
            o_ref[g, q0 + r0:q0 + r0 + half, :] = o.astype(BF16)


def _band_attn_prompt(qkv_hm, bias, batch, seq):
    hps = HEADS_PER_STEP
    per_part = N_HEADS // hps
    head = lambda part: pl.BlockSpec((hps, seq, HEAD_DIM), lambda h, b: (part * per_part + h, b, 0))
    return pl.pallas_call(
        _band_attn_prompt_kernel,
        out_shape=jax.ShapeDtypeStruct((N_HEADS, batch * seq, HEAD_DIM), BF16),
        grid=(per_part, batch),
        in_specs=[head(0), head(1), head(2),
                  pl.BlockSpec((hps,) + bias.shape[1:], lambda h, b: (h, 0, 0))],
        out_specs=head(0),
        compiler_params=_params(("arbitrary", "arbitrary")),
        name="band_attn_prompt",
    )(qkv_hm, qkv_hm, qkv_hm, bias)


def _band_attn_sample_kernel(qkv_ref, ck_ref, cv_ref, bias_c_ref, bias_n_ref, o_ref):
    k_heads = pltpu.einshape("thd->htd", ck_ref[...]).astype(BF16)
    v_heads = pltpu.einshape("thd->htd", cv_ref[...]).astype(BF16)
    for h in range(N_HEADS):
        q, kn, vn = qkv_ref[h], qkv_ref[N_HEADS + h], qkv_ref[2 * N_HEADS + h]
        kc = k_heads[h]
        vc = v_heads[h]
        s_c = _qk(q, kc) + bias_c_ref[h]
        s_n = _qk(q, kn) + bias_n_ref[h]
        mx = jnp.maximum(jnp.max(s_c, axis=-1, keepdims=True), jnp.max(s_n, axis=-1, keepdims=True))
        p_c = jnp.exp2(s_c - mx)
        p_n = jnp.exp2(s_n - mx)
        l = jnp.sum(p_c, axis=-1, keepdims=True) + jnp.sum(p_n, axis=-1, keepdims=True)
        o = (jnp.dot(p_c.astype(BF16), vc, preferred_element_type=F32)
             + jnp.dot(p_n.astype(BF16), vn, preferred_element_type=F32)) / l
        o_ref[h] = o.astype(BF16)


def _band_attn_sample(qkv_hm, ck, cv, bias_c, bias_n, batch, seq):
    cache = pl.BlockSpec((None, PAST_WIN, N_HEADS, HEAD_DIM), lambda b: (b, 0, 0, 0))
    const = lambda a: pl.BlockSpec(a.shape, lambda b: (0,) * a.ndim, pipeline_mode=pl.Buffered(1))
    return pl.pallas_call(
        _band_attn_sample_kernel,
        out_shape=jax.ShapeDtypeStruct((N_HEADS, batch * seq, HEAD_DIM), BF16),
        grid=(batch,),
        in_specs=[
            pl.BlockSpec((3 * N_HEADS, seq, HEAD_DIM), lambda b: (0, b, 0)),
            cache, cache,
            const(bias_c), const(bias_n),
        ],
        out_specs=pl.BlockSpec((N_HEADS, seq, HEAD_DIM), lambda b: (0, b, 0)),
        compiler_params=_params(("arbitrary",)),
        name="band_attn_sample",
    )(qkv_hm, ck, cv, bias_c, bias_n)


def _band_bias_table(rel_bias):
    nh = rel_bias.shape[0]
    n_keys = PAST_WIN + QUERY_GROUP
    period = n_keys + QUERY_GROUP
    n_far_past = PAST_WIN - MAX_REL + QUERY_GROUP - 1
    n_future = period - n_far_past - rel_bias.shape[1]
    f = jnp.concatenate([
        jnp.broadcast_to(rel_bias[:, -1:], (nh, n_far_past)),
        rel_bias[:, ::-1],
        jnp.broadcast_to(rel_bias[:, :1], (nh, n_future)),
    ], axis=1)
    f = jnp.roll(f, -(QUERY_GROUP - 1), axis=1)
    flat = jnp.tile(f, (1, QUERY_GROUP))[:, :QUERY_GROUP * (period - 1)]
    toeplitz = flat.reshape(nh, QUERY_GROUP, period - 1)[:, :, :n_keys]
    qi = jnp.arange(QUERY_GROUP)[:, None]
    kj = jnp.arange(n_keys)[None, :]
    band_lo = (qi // CHUNK) * CHUNK
    in_band = (kj >= band_lo) & (kj < band_lo + BAND)
    return jnp.where(in_band[None], toeplitz * LOG2_E, NEG_INF).astype(F32)


def _mem_attn_kernel(xn_ref, wq_ref, mk_ref, mv_ref, wo_ref, x_ref, gp_ref, gn_ref,
                     xo_ref, xno_ref, acc0_ref, acc1_ref, *, n_tiles):
    def attend():
        q = jnp.dot(xn_ref[...], wq_ref[...], preferred_element_type=F32) * (MEM_HEAD_DIM ** -0.5 * LOG2_E)
        n_streams = mk_ref.shape[0]
        rows_per_stream = q.shape[0] // n_streams
        streams = []
        for b in range(n_streams):
            rows = slice(b * rows_per_stream, (b + 1) * rows_per_stream)
            heads = []
            for h in range(MEM_HEADS):
                c = slice(h * MEM_HEAD_DIM, (h + 1) * MEM_HEAD_DIM)
                head_rows = pl.ds(h, N_MEM, stride=MEM_HEADS)
                s = _qk(q[rows, c].astype(BF16), mk_ref[b, head_rows, :].astype(BF16))
                p = jnp.exp2(s - jnp.max(s, axis=-1, keepdims=True))
                l = jnp.sum(p, axis=-1, keepdims=True)
                o = jnp.dot(p.astype(BF16), mv_ref[b, head_rows, :].astype(BF16), preferred_element_type=F32) / l
                heads.append(o.astype(BF16))
            streams.append(jnp.concatenate(heads, axis=1))
        return streams[0] if n_streams == 1 else jnp.concatenate(streams, axis=0)

    _skewed_step(attend, wo_ref, x_ref, gp_ref, gn_ref, xo_ref, xno_ref, acc0_ref, acc1_ref, 1.0, n_tiles)


def _mem_attn(xn, x, mk, mv, wq, wo, layer, g4, batch, tm):
    t, d = x.shape
    n = t // tm
    seq = t // batch
    streams_per_tile = max(1, tm // seq)
    assert tm * n == t and (seq % tm == 0 or tm % seq == 0)
    cur = lambda i: (jnp.minimum(i, n - 1), 0)
    prev = lambda i: (jnp.maximum(i - 1, 0), 0)
    mem_spec = pl.BlockSpec(
        (None, streams_per_tile, N_MEM * MEM_HEADS, MEM_HEAD_DIM),
        lambda i: (layer, (jnp.minimum(i, n - 1) * tm) // (streams_per_tile * seq), 0, 0))
    return pl.pallas_call(
        functools.partial(_mem_attn_kernel, n_tiles=n),
        out_shape=[jax.ShapeDtypeStruct((t, d), F32), jax.ShapeDtypeStruct((t, d), BF16)],
        grid=(n + 1,),
        in_specs=[
            pl.BlockSpec((tm, d), cur),
            _layer_spec((d, MEM_WIDTH), layer),
            mem_spec, mem_spec,
            _layer_spec((MEM_WIDTH, d), layer),
            pl.BlockSpec((tm, d), prev),
            _gain_spec(layer, 5),
            _gain_spec(layer, 6),
        ],
        out_specs=[pl.BlockSpec((tm, d), prev), pl.BlockSpec((tm, d), prev)],
        scratch_shapes=[pltpu.VMEM((tm, d), F32), pltpu.VMEM((tm, d), F32)],
        compiler_params=_params(("arbitrary",)),
        name="mem_attn",
    )(xn, wq, mk, mv, wo, x, g4, g4)


def _mem_project_kernel(mem_ref, g_ref, wf_ref, k_ref, v_ref, w_ref):
    @pl.when(pl.program_id(1) == 0)
    def _():
        w_ref[...] = wf_ref[...].astype(BF16)

    kv = jnp.dot(_rms(mem_ref[...], g_ref[...]).astype(BF16), w_ref[...], preferred_element_type=F32)
    tm = kv.shape[0]
    for h in range(MEM_HEADS):
        head_rows = pl.ds(h, tm, stride=MEM_HEADS)
        k_ref[head_rows, :] = kv[:, h * MEM_HEAD_DIM:(h + 1) * MEM_HEAD_DIM]
        v_ref[head_rows, :] = kv[:, MEM_WIDTH + h * MEM_HEAD_DIM:MEM_WIDTH + (h + 1) * MEM_HEAD_DIM]


def _mem_project(mem, g_mem, w_kv, tm):
    t, d = mem.shape
    depth = w_kv.shape[0]
    out = jax.ShapeDtypeStruct((depth, t * MEM_HEADS, MEM_HEAD_DIM), F32)
    out_spec = pl.BlockSpec((None, tm * MEM_HEADS, MEM_HEAD_DIM), lambda l, i: (l, i, 0))
    return pl.pallas_call(
        _mem_project_kernel,
        out_shape=[out, out],
        grid=(depth, t // tm),
        in_specs=[
            pl.BlockSpec((tm, d), lambda l, i: (i, 0)),
            pl.BlockSpec((None, 1, d), lambda l, i: (l, 0, 0)),
            pl.BlockSpec((None, d, 2 * MEM_WIDTH), lambda l, i: (l, 0, 0)),
        ],
        out_specs=[out_spec, out_spec],
        scratch_shapes=[pltpu.VMEM((d, 2 * MEM_WIDTH), BF16)],
        compiler_params=_params(("arbitrary", "arbitrary")),
        name="mem_project",
    )(mem, g_mem, w_kv)


def _trunk(x, mem_k, mem_v, weights, *, batch, seq, conv_hist, band_cache, bias_tables, ffn_out_bf16=None):
    prompt = conv_hist is None
    (g4, w_ffn1_in, w_ffn1_out, w_ffn2_in, w_ffn2_out, w_conv_in, w_conv_dw, w_conv_out,
     w_attn_qkv, w_attn_o, w_mem_q, w_mem_o) = weights
    depth = g4.shape[0]
    tm_mem = TM_MEM if prompt else TM_MEM_SAMPLE
    cast_here = ffn_out_bf16 is None
    if cast_here:
        ffn_out_bf16 = []

    def ffn_in(xn, w_in, w_out, layer, which):
        if cast_here:
            a, w_out_bf16 = _ffn_in(xn, w_in, layer, w_out)
            ffn_out_bf16.append(w_out_bf16)
            return a, w_out_bf16
        return _ffn_in(xn, w_in, layer), ffn_out_bf16[2 * layer + which]

    xn = _rms_cast(x, g4)
    conv_state = band_k = band_v = None
    for i in range(depth):
        m = i // 2
        a, w_out = ffn_in(xn, w_ffn1_in, w_ffn1_out, i, 0)
        x, xn = _mm_res_norm(a, w_out, 0, x, g4, (i, 1), (i, 2), 0.5, TM_FFN_OUT)
        if i % 2 == 0:
            if prompt:
                z, conv_state = _conv_in_prompt(xn, w_conv_in, w_conv_dw, m, batch, seq)
            else:
                z, conv_state = _conv_in_sample(xn, w_conv_in, w_conv_dw, m, conv_hist, seq)
            x, xn = _mm_res_norm(z, w_conv_out, m, x, g4, (i, 3), (i, 4), 1.0, TM_MIX_OUT)
        else:
            if prompt:
                qkv_hm, band_k, band_v = _qkv(xn, w_attn_qkv, m, seq // TM_QKV)
                o = _band_attn_prompt(qkv_hm, bias_tables[0], batch, seq)
            else:
                qkv_hm, band_k, band_v = _qkv(xn, w_attn_qkv, m, 1)
                o = _band_attn_sample(qkv_hm, band_cache[0], band_cache[1], bias_tables[1], bias_tables[2],
                                      batch, seq)
            x, xn = _mm_res_norm(o, w_attn_o, m, x, g4, (i, 3), (i, 4), 1.0, TM_MIX_OUT)
        x, xn = _mem_attn(xn, x, mem_k, mem_v, w_mem_q, w_mem_o, i, g4, batch, tm_mem)
        a, w_out = ffn_in(xn, w_ffn2_in, w_ffn2_out, i, 1)
        last = i == depth - 1
        x, xn = _mm_res_norm(a, w_out, 0, x, g4, (i, 7), ((i + 1) % depth, 0), 0.5, TM_FFN_OUT,
                             emit_xn=not last)
    return x, conv_state, band_k, band_v, ffn_out_bf16


def kernel(x_prompt, x_sample, state_conv, cache_band_k, cache_band_v, cache_mem_k, cache_mem_v, mem_prompt, g_norm, g_mem, w_ffn1_in, w_ffn1_out, w_ffn2_in, w_ffn2_out, w_conv_in, w_conv_dw, w_conv_out, w_attn_qkv, rel_bias, w_attn_o, w_mem_q, w_mem_kv, w_mem_o):
    batch, seq, d = x_prompt.shape
    dec_batch, dec_seq, _ = x_sample.shape
    depth = g_norm.shape[0]
    assert depth == 2 and state_conv.shape[0] == 1 and cache_band_k.shape[0] == 1
    assert cache_band_k.shape[2] == PAST_WIN and dec_seq == CHUNK and seq % QUERY_GROUP == 0
    assert seq % TM_IN == 0 and seq >= PAST_WIN and TM_IN % dec_seq == 0 and TM_QKV % dec_seq == 0

    weights = (
        g_norm[:, :, None, :],
        w_ffn1_in, w_ffn1_out, w_ffn2_in, w_ffn2_out,
        w_conv_in, w_conv_dw, w_conv_out.astype(BF16),
        w_attn_qkv.astype(BF16), w_attn_o.astype(BF16), w_mem_q.astype(BF16), w_mem_o.astype(BF16),
    )
    table = _band_bias_table(rel_bias[0])
    bias_tables = (table, table[:, :CHUNK, :PAST_WIN], table[:, :CHUNK, PAST_WIN:BAND])

    mem_rows = N_MEM * MEM_HEADS
    mem_k_p, mem_v_p = _mem_project(mem_prompt.reshape(batch * N_MEM, d), g_mem[:, None, :], w_mem_kv, 512)

    y_p, conv_p, bk_p, bv_p, ffn_out_bf16 = _trunk(
        x_prompt.reshape(batch * seq, d),
        mem_k_p.reshape(depth, batch, mem_rows, MEM_HEAD_DIM),
        mem_v_p.reshape(depth, batch, mem_rows, MEM_HEAD_DIM), weights,
        batch=batch, seq=seq, conv_hist=None, band_cache=None, bias_tables=bias_tables)

    y_s, conv_s, bk_s, bv_s, _ = _trunk(
        x_sample.reshape(dec_batch * dec_seq, d),
        cache_mem_k.reshape(depth, dec_batch, mem_rows, MEM_HEAD_DIM),
        cache_mem_v.reshape(depth, dec_batch, mem_rows, MEM_HEAD_DIM), weights,
        batch=dec_batch, seq=dec_seq, conv_hist=state_conv[0],
        band_cache=(cache_band_k[0], cache_band_v[0]),
        bias_tables=bias_tables, ffn_out_bf16=ffn_out_bf16)

    mem_shape = (depth, batch, N_MEM, MEM_HEADS, MEM_HEAD_DIM)
    band_p_shape = (1, batch, PAST_WIN, N_HEADS, HEAD_DIM)
    band_s_shape = (1, dec_batch, dec_seq, N_HEADS, HEAD_DIM)
    return (y_p.reshape(batch, seq, d), y_s.reshape(dec_batch, dec_seq, d),
            conv_p[None], bk_p.reshape(band_p_shape), bv_p.reshape(band_p_shape),
            mem_k_p.reshape(mem_shape), mem_v_p.reshape(mem_shape),
            conv_s[None], bk_s.reshape(band_s_shape), bv_s.reshape(band_s_shape))
```

```python
import functools

import jax
import jax.numpy as jnp
from jax import lax
from jax.experimental import pallas as pl
from jax.experimental.pallas import tpu as pltpu

F32 = jnp.float32
BF16 = jnp.bfloat16

D_MODEL = 2048
D_FF = 5632
N_HEADS = 16
HEAD_DIM = 128
CHUNK = 64
PAST_WIN = 512
BAND = PAST_WIN + CHUNK
MAX_REL = 128
N_MEM = 256
MEM_HEADS = 4
MEM_HEAD_DIM = 128
MEM_WIDTH = MEM_HEADS * MEM_HEAD_DIM
EPS = 1e-6
NEG_INF = -1e30
LOG2_E = 1.4426950408889634

V7X_VMEM_LIMIT_BYTES = 58 * 1024 * 1024
BF16_SUBLANES = 16
LANES = 128
EPILOGUE_PIECES = 4
FF_TILE = 512
CONV_TILE = 512
TM_QKV = PAST_WIN
QUERY_GROUP = 4 * CHUNK
HEADS_PER_STEP = 4
TM_IN = 1024
TM_FFN_IN = 2048
TM_FFN_OUT = 256
TM_MIX_OUT = 512
TM_MEM = 512
TM_MEM_SAMPLE = 256


def _params(sem):
    return pltpu.CompilerParams(dimension_semantics=sem, vmem_limit_bytes=V7X_VMEM_LIMIT_BYTES)


def _layer_spec(shape, layer):
    zeros = (0,) * len(shape)
    return pl.BlockSpec((None,) + tuple(shape), lambda *_: (layer,) + zeros, pipeline_mode=pl.Buffered(1))


def _gain_spec(layer, n):
    return pl.BlockSpec((None, None, 1, D_MODEL), lambda *_: (layer, n, 0, 0), pipeline_mode=pl.Buffered(1))


def _rms(x, g):
    return x * lax.rsqrt(jnp.mean(x * x, axis=-1, keepdims=True) + EPS) * g


def _rms_cast_kernel(x_ref, g_ref, o_ref):
    o_ref[...] = _rms(x_ref[...], g_ref[...]).astype(BF16)


def _rms_cast(x, g4):
    t, d = x.shape
    return pl.pallas_call(
        _rms_cast_kernel,
        out_shape=jax.ShapeDtypeStruct((t, d), BF16),
        grid=(t // TM_IN,),
        in_specs=[pl.BlockSpec((TM_IN, d), lambda i: (i, 0)), _gain_spec(0, 0)],
        out_specs=pl.BlockSpec((TM_IN, d), lambda i: (i, 0)),
        compiler_params=_params(("arbitrary",)),
        name="rms_cast",
    )(x, g4)


def _res_norm_epilogue(acc_ref, x_ref, gp, gn, xo_ref, xno_ref, row0, n_rows):
    rows = BF16_SUBLANES
    xnew = None
    for r in range(row0, row0 + n_rows, rows):
        sl = slice(r, r + rows)
        xnew = x_ref[sl, :] + _rms(acc_ref[sl, :], gp)
        xo_ref[sl, :] = xnew
        if xno_ref is not None:
            xno_ref[sl, :] = _rms(xnew, gn).astype(BF16)
    return xnew[:, :LANES]


def _zero_from(x):
    sixteen = jnp.uint32(16)
    bits = lax.bitcast_convert_type(x, jnp.uint32)
    return lax.shift_right_logical(lax.shift_right_logical(bits, sixteen), sixteen).astype(F32)


def _after(lhs, zero):
    r = BF16_SUBLANES
    first = (lhs[:r, :LANES].astype(F32) + zero).astype(lhs.dtype)
    top = jnp.concatenate([first, lhs[:r, LANES:]], axis=1)
    return jnp.concatenate([top, lhs[r:, :]], axis=0)


def _skewed_step(load_lhs, w_ref, x_ref, gp_ref, gn_ref, xo_ref, xno_ref, acc0_ref, acc1_ref, scale, n_tiles):
    i = pl.program_id(0)
    tm, d = acc0_ref.shape
    rows = tm // EPILOGUE_PIECES
    cols = d // EPILOGUE_PIECES
    accs = (acc0_ref, acc1_ref)

    @pl.when(i == 0)
    def _():
        acc1_ref[...] = jnp.zeros_like(acc1_ref)

    def gains():
        return gp_ref[...] * scale, gn_ref[...]

    def step(prev_ref, cur_ref):
        gp, gn = gains()
        lhs = load_lhs()
        for k in range(EPILOGUE_PIECES):
            c = slice(k * cols, (k + 1) * cols)
            cur_ref[:, c] = jnp.dot(lhs, w_ref[:, c], preferred_element_type=F32)
            done = _res_norm_epilogue(prev_ref, x_ref, gp, gn, xo_ref, xno_ref, k * rows, rows)
            lhs = _after(lhs, _zero_from(done))

    for parity in (0, 1):
        @pl.when((i % 2 == parity) & (i < n_tiles))
        def _():
            step(accs[1 - parity], accs[parity])

    @pl.when(i == n_tiles)
    def _():
        gp, gn = gains()
        _res_norm_epilogue(accs[(n_tiles - 1) % 2], x_ref, gp, gn, xo_ref, xno_ref, 0, tm)


def _load_rows(a_ref):
    if len(a_ref.shape) == 2:
        return a_ref[...]
    return jnp.concatenate([a_ref[h] for h in range(a_ref.shape[0])], axis=1)


def _mm_res_norm_kernel(a_ref, w_ref, x_ref, gp_ref, gn_ref, xo_ref, *rest, scale, emit_xn, n_tiles):
    xno_ref, acc0_ref, acc1_ref = rest if emit_xn else (None,) + rest
    _skewed_step(lambda: _load_rows(a_ref), w_ref, x_ref, gp_ref, gn_ref, xo_ref, xno_ref,
                 acc0_ref, acc1_ref, scale, n_tiles)


def _mm_res_norm(a, w, w_layer, x, g4, post_gain, next_gain, scale, tm, emit_xn=True):
    t, d = x.shape
    k = w.shape[1]
    n = t // tm
    cur = lambda i: jnp.minimum(i, n - 1)
    prev = lambda i: (jnp.maximum(i - 1, 0), 0)
    if a.ndim == 2:
        a_spec = pl.BlockSpec((tm, k), lambda i: (cur(i), 0))
    else:
        a_spec = pl.BlockSpec((a.shape[0], tm, a.shape[2]), lambda i: (0, cur(i), 0))
    out_shape = [jax.ShapeDtypeStruct((t, d), F32)]
    out_specs = [pl.BlockSpec((tm, d), prev)]
    if emit_xn:
        out_shape.append(jax.ShapeDtypeStruct((t, d), BF16))
        out_specs.append(pl.BlockSpec((tm, d), prev))
    outs = pl.pallas_call(
        functools.partial(_mm_res_norm_kernel, scale=scale, emit_xn=emit_xn, n_tiles=n),
        out_shape=out_shape,
        grid=(n + 1,),
        in_specs=[
            a_spec,
            _layer_spec((k, d), w_layer),
            pl.BlockSpec((tm, d), prev),
            _gain_spec(*post_gain),
            _gain_spec(*next_gain),
        ],
        out_specs=out_specs,
        scratch_shapes=[pltpu.VMEM((tm, d), F32), pltpu.VMEM((tm, d), F32)],
        compiler_params=_params(("arbitrary",)),
        name="mm_res_norm",
    )(a, w, x, g4, g4)
    return (outs[0], outs[1]) if emit_xn else (outs[0], None)


def _ffn_in_kernel(xn_ref, wg_ref, wu_ref, *rest, cast_w_out):
    if cast_w_out:
        wo_f32_ref, o_ref, wo_bf16_ref, w_ref = rest
        wo_bf16_ref[...] = wo_f32_ref[...].astype(BF16)
    else:
        o_ref, w_ref = rest

    @pl.when(pl.program_id(1) == 0)
    def _():
        w_ref[:, :FF_TILE] = wg_ref[...].astype(BF16)
        w_ref[:, FF_TILE:] = wu_ref[...].astype(BF16)

    half = xn_ref.shape[0] // 2
    for r in (slice(0, half), slice(half, 2 * half)):
        h = jnp.dot(xn_ref[r, :], w_ref[...], preferred_element_type=F32)
        o_ref[r, :] = (jax.nn.silu(h[:, :FF_TILE]) * h[:, FF_TILE:]).astype(BF16)


def _ffn_in(xn, w_in, layer, w_out=None):
    t, d = xn.shape
    nj = D_FF // FF_TILE
    ni = t // TM_FFN_IN
    in_specs = [
        pl.BlockSpec((TM_FFN_IN, d), lambda j, i: (i, 0)),
        pl.BlockSpec((None, d, FF_TILE), lambda j, i: (layer, 0, j)),
        pl.BlockSpec((None, d, FF_TILE), lambda j, i: (layer, 0, nj + j)),
    ]
    out_shape = [jax.ShapeDtypeStruct((t, D_FF), BF16)]
    out_specs = [pl.BlockSpec((TM_FFN_IN, FF_TILE), lambda j, i: (i, j))]
    operands = [xn, w_in, w_in]
    if w_out is not None:
        rows = D_FF // (nj * ni)
        assert rows * nj * ni == D_FF and rows % BF16_SUBLANES == 0
        in_specs.append(pl.BlockSpec((None, rows, d), lambda j, i: (layer, j * ni + i, 0)))
        out_shape.append(jax.ShapeDtypeStruct((1, D_FF, d), BF16))
        out_specs.append(pl.BlockSpec((None, rows, d), lambda j, i: (0, j * ni + i, 0)))
        operands.append(w_out)
    outs = pl.pallas_call(
        functools.partial(_ffn_in_kernel, cast_w_out=w_out is not None),
        out_shape=out_shape,
        grid=(nj, ni),
        in_specs=in_specs,
        out_specs=out_specs,
        scratch_shapes=[pltpu.VMEM((d, 2 * FF_TILE), BF16)],
        compiler_params=_params(("arbitrary", "arbitrary")),
        name="ffn_in",
    )(*operands)
    return outs if w_out is not None else outs[0]


def _qkv_kernel(xn_ref, w_ref, hm_ref, k_ref, v_ref, *, keep_every):
    j, i = pl.program_id(0), pl.program_id(1)
    acc = jnp.dot(xn_ref[...], w_ref[...], preferred_element_type=F32)
    heads = [acc[:, h * HEAD_DIM:(h + 1) * HEAD_DIM] for h in range(N_HEADS)]
    hm_scale = jnp.where(j == 0, HEAD_DIM ** -0.5 * LOG2_E, 1.0)
    for h in range(N_HEADS):
        hm_ref[h] = (heads[h] * hm_scale).astype(BF16)

    def keep(ref):
        ref[...] = pltpu.einshape("htd->thd", jnp.stack(heads))

    kept_tile = i % keep_every == keep_every - 1

    @pl.when((j == 1) & kept_tile)
    def _():
        keep(k_ref)

    @pl.when((j == 2) & kept_tile)
    def _():
        keep(v_ref)


def _qkv(xn, w_qkv, layer, keep_every):
    t, d = xn.shape
    n_keep = t // (TM_QKV * keep_every)
    kept = jax.ShapeDtypeStruct((n_keep, TM_QKV, N_HEADS, HEAD_DIM), F32)
    kept_block = (None, TM_QKV, N_HEADS, HEAD_DIM)

    def k_index(j, i):
        return (jnp.where(j < 1, 0, jnp.where(j > 1, n_keep - 1, i // keep_every)), 0, 0, 0)

    def v_index(j, i):
        return (jnp.where(j < 2, 0, i // keep_every), 0, 0, 0)

    return pl.pallas_call(
        functools.partial(_qkv_kernel, keep_every=keep_every),
        out_shape=[jax.ShapeDtypeStruct((3 * N_HEADS, t, HEAD_DIM), BF16), kept, kept],
        grid=(3, t // TM_QKV),
        in_specs=[
            pl.BlockSpec((TM_QKV, d), lambda j, i: (i, 0)),
            pl.BlockSpec((None, d, D_MODEL), lambda j, i: (layer, 0, j), pipeline_mode=pl.Buffered(1)),
        ],
        out_specs=[
            pl.BlockSpec((N_HEADS, TM_QKV, HEAD_DIM), lambda j, i: (j, i, 0)),
            pl.BlockSpec(kept_block, k_index),
            pl.BlockSpec(kept_block, v_index),
        ],
        compiler_params=_params(("arbitrary", "arbitrary")),
        name="qkv",
    )(xn, w_qkv)


def _cast_conv_weights(wb_ref, wc_ref, wx_ref, w_ref):
    @pl.when(pl.program_id(1) == 0)
    def _():
        for c in range(CONV_TILE // LANES):
            lane = slice(c * LANES, (c + 1) * LANES)
            for part, ref in enumerate((wb_ref, wc_ref, wx_ref)):
                w_ref[:, (3 * c + part) * LANES:(3 * c + part + 1) * LANES] = ref[:, lane].astype(BF16)


def _gated_conv(h, hist0, hist1, wdw, seq):
    tm = h.shape[0]
    nseq = tm // seq
    row = lax.broadcasted_iota(jnp.int32, (nseq, seq, LANES), 1)
    zs, us = [], []
    for c in range(CONV_TILE // LANES):
        lane = slice(c * LANES, (c + 1) * LANES)
        bg, cg, xv = (h[:, (3 * c + part) * LANES:(3 * c + part + 1) * LANES] for part in range(3))
        u = cg * xv
        u3 = u.reshape(nseq, seq, LANES)
        prev1 = pltpu.roll(u, 1, axis=0).reshape(u3.shape)
        prev2 = pltpu.roll(u, 2, axis=0).reshape(u3.shape)
        prev1 = jnp.where(row == 0, hist1[:, :, lane], prev1)
        prev2 = jnp.where(row == 0, hist0[:, :, lane], jnp.where(row == 1, hist1[:, :, lane], prev2))
        y = wdw[0:1, lane] * prev2 + wdw[1:2, lane] * prev1 + wdw[2:3, lane] * u3
        zs.append((bg.reshape(u3.shape) * y).astype(BF16).reshape(tm, LANES))
        us.append(u3)
    return jnp.concatenate(zs, axis=1), jnp.concatenate(us, axis=2)


def _conv_in_prompt_kernel(xn_ref, wb_ref, wc_ref, wx_ref, wdw_ref, z_ref, st_ref, w_ref, carry_ref,
                           *, tiles_per_seq):
    tm = xn_ref.shape[0]
    _cast_conv_weights(wb_ref, wc_ref, wx_ref, w_ref)

    @pl.when(pl.program_id(1) % tiles_per_seq == 0)
    def _():
        carry_ref[...] = jnp.zeros_like(carry_ref)

    h = jnp.dot(xn_ref[...], w_ref[...], preferred_element_type=F32)
    hist0 = carry_ref[0:1, :].reshape(1, 1, CONV_TILE)
    hist1 = carry_ref[1:2, :].reshape(1, 1, CONV_TILE)
    z, u3 = _gated_conv(h, hist0, hist1, wdw_ref[...], tm)
    z_ref[...] = z
    last = u3[0, tm - 2:, :]
    carry_ref[0:2, :] = last
    st_ref[0] = last


def _conv_weight_specs(d, layer):
    nj = d // CONV_TILE
    part = lambda p: pl.BlockSpec((None, d, CONV_TILE), lambda j, i: (layer, 0, p * nj + j),
                                  pipeline_mode=pl.Buffered(1))
    return [part(0), part(1), part(2), pl.BlockSpec((None, 3, CONV_TILE), lambda j, i: (layer, 0, j))]


def _conv_in_prompt(xn, w_in, w_dw, layer, batch, seq):
    t, d = xn.shape
    tiles_per_seq = seq // TM_IN
    return pl.pallas_call(
        functools.partial(_conv_in_prompt_kernel, tiles_per_seq=tiles_per_seq),
        out_shape=[jax.ShapeDtypeStruct((t, d), BF16), jax.ShapeDtypeStruct((batch, 2, d), F32)],
        grid=(d // CONV_TILE, t // TM_IN),
        in_specs=[pl.BlockSpec((TM_IN, d), lambda j, i: (i, 0))] + _conv_weight_specs(d, layer),
        out_specs=[
            pl.BlockSpec((TM_IN, CONV_TILE), lambda j, i: (i, j)),
            pl.BlockSpec((1, 2, CONV_TILE), lambda j, i: (i // tiles_per_seq, 0, j)),
        ],
        scratch_shapes=[pltpu.VMEM((d, 3 * CONV_TILE), BF16), pltpu.VMEM((8, CONV_TILE), F32)],
        compiler_params=_params(("arbitrary", "arbitrary")),
        name="conv_in_prompt",
    )(xn, w_in, w_in, w_in, w_dw)


def _conv_in_sample_kernel(xn_ref, wb_ref, wc_ref, wx_ref, wdw_ref, hist_ref, z_ref, st_ref, w_ref, *, seq):
    _cast_conv_weights(wb_ref, wc_ref, wx_ref, w_ref)
    h = jnp.dot(xn_ref[...], w_ref[...], preferred_element_type=F32)
    hist = hist_ref[...]
    z, u3 = _gated_conv(h, hist[:, 0:1, :], hist[:, 1:2, :], wdw_ref[...], seq)
    z_ref[...] = z
    st_ref[...] = u3[:, seq - 2:, :]


def _conv_in_sample(xn, w_in, w_dw, layer, hist, seq):
    t, d = xn.shape
    nseq = TM_IN // seq
    hist_spec = pl.BlockSpec((nseq, 2, CONV_TILE), lambda j, i: (i, 0, j))
    return pl.pallas_call(
        functools.partial(_conv_in_sample_kernel, seq=seq),
        out_shape=[jax.ShapeDtypeStruct((t, d), BF16), jax.ShapeDtypeStruct(hist.shape, F32)],
        grid=(d // CONV_TILE, t // TM_IN),
        in_specs=[pl.BlockSpec((TM_IN, d), lambda j, i: (i, 0))] + _conv_weight_specs(d, layer) + [hist_spec],
        out_specs=[pl.BlockSpec((TM_IN, CONV_TILE), lambda j, i: (i, j)), hist_spec],
        scratch_shapes=[pltpu.VMEM((d, 3 * CONV_TILE), BF16)],
        compiler_params=_params(("arbitrary", "arbitrary")),
        name="conv_in_sample",
    )(xn, w_in, w_in, w_in, w_dw, hist)


def _qk(q, k):
    return lax.dot_general(q, k, (((1,), (1,)), ((), ())), preferred_element_type=F32)


def _band_attn_prompt_kernel(q_ref, k_ref, v_ref, bias_ref, o_ref):
    seq = q_ref.shape[1]
    items = [(g, m) for g in range(q_ref.shape[0]) for m in range(seq // QUERY_GROUP)]

    def key_range(m):
        q0 = m * QUERY_GROUP
        return max(0, q0 - PAST_WIN), q0 + QUERY_GROUP

    def scores(g, m):
        q0 = m * QUERY_GROUP
        lo, hi = key_range(m)
        col0 = lo - (q0 - PAST_WIN)
        return _qk(q_ref[g, q0:hi, :], k_ref[g, lo:hi, :]) + bias_ref[g, :, col0:col0 + hi - lo]

    s_next = scores(*items[0])
    for n, (g, m) in enumerate(items):
        s = s_next
        if n + 1 < len(items):
            s_next = scores(*items[n + 1])
        lo, hi = key_range(m)
        q0 = m * QUERY_GROUP
        col0 = lo - (q0 - PAST_WIN)
        half = QUERY_GROUP // 2
        for r0 in (0, half):
            c_lo = max(r0 - col0, 0)
            c_hi = min(r0 + PAST_WIN + half, PAST_WIN + QUERY_GROUP) - col0
            sp = s[r0:r0 + half, c_lo:c_hi]
            p = jnp.exp2(sp - jnp.max(sp, axis=-1, keepdims=True))
            l = jnp.sum(p, axis=-1, keepdims=True)
            o = jnp.dot(p.astype(BF16), v_ref[g, lo + c_lo:lo + c_hi, :], preferred_element_type=F32) / l
            o_ref[g, q0 + r0:q0 + r0 + half, :] = o.astype(BF16)


def _band_attn_prompt(qkv_hm, bias, batch, seq):
    hps = HEADS_PER_STEP
    per_part = N_HEADS // hps
    head = lambda part: pl.BlockSpec((hps, seq, HEAD_DIM), lambda h, b: (part * per_part + h, b, 0))
    return pl.pallas_call(
        _band_attn_prompt_kernel,
        out_shape=jax.ShapeDtypeStruct((N_HEADS, batch * seq, HEAD_DIM), BF16),
        grid=(per_part, batch),
        in_specs=[head(0), head(1), head(2),
                  pl.BlockSpec((hps,) + bias.shape[1:], lambda h, b: (h, 0, 0))],
        out_specs=head(0),
        compiler_params=_params(("arbitrary", "arbitrary")),
        name="band_attn_prompt",
    )(qkv_hm, qkv_hm, qkv_hm, bias)


def _band_attn_sample_kernel(qkv_ref, ck_ref, cv_ref, bias_c_ref, bias_n_ref, o_ref):
    k_heads = pltpu.einshape("thd->htd", ck_ref[...]).astype(BF16)
    v_heads = pltpu.einshape("thd->htd", cv_ref[...]).astype(BF16)
    for h in range(N_HEADS):
        q, kn, vn = qkv_ref[h], qkv_ref[N_HEADS + h], qkv_ref[2 * N_HEADS + h]
        kc = k_heads[h]
        vc = v_heads[h]
        s_c = _qk(q, kc) + bias_c_ref[h]
        s_n = _qk(q, kn) + bias_n_ref[h]
        mx = jnp.maximum(jnp.max(s_c, axis=-1, keepdims=True), jnp.max(s_n, axis=-1, keepdims=True))
        p_c = jnp.exp2(s_c - mx)
        p_n = jnp.exp2(s_n - mx)
        l = jnp.sum(p_c, axis=-1, keepdims=True) + jnp.sum(p_n, axis=-1, keepdims=True)
        o = (jnp.dot(p_c.astype(BF16), vc, preferred_element_type=F32)
             + jnp.dot(p_n.astype(BF16), vn, preferred_element_type=F32)) / l
        o_ref[h] = o.astype(BF16)


def _band_attn_sample(qkv_hm, ck, cv, bias_c, bias_n, batch, seq):
    cache = pl.BlockSpec((None, PAST_WIN, N_HEADS, HEAD_DIM), lambda b: (b, 0, 0, 0))
    const = lambda a: pl.BlockSpec(a.shape, lambda b: (0,) * a.ndim, pipeline_mode=pl.Buffered(1))
    return pl.pallas_call(
        _band_attn_sample_kernel,
        out_shape=jax.ShapeDtypeStruct((N_HEADS, batch * seq, HEAD_DIM), BF16),
        grid=(batch,),
        in_specs=[
            pl.BlockSpec((3 * N_HEADS, seq, HEAD_DIM), lambda b: (0, b, 0)),
            cache, cache,
            const(bias_c), const(bias_n),
        ],
        out_specs=pl.BlockSpec((N_HEADS, seq, HEAD_DIM), lambda b: (0, b, 0)),
        compiler_params=_params(("arbitrary",)),
        name="band_attn_sample",
    )(qkv_hm, ck, cv, bias_c, bias_n)


def _band_bias_table(rel_bias):
    nh = rel_bias.shape[0]
    n_keys = PAST_WIN + QUERY_GROUP
    period = n_keys + QUERY_GROUP
    n_far_past = PAST_WIN - MAX_REL + QUERY_GROUP - 1
    n_future = period - n_far_past - rel_bias.shape[1]
    f = jnp.concatenate([
        jnp.broadcast_to(rel_bias[:, -1:], (nh, n_far_past)),
        rel_bias[:, ::-1],
        jnp.broadcast_to(rel_bias[:, :1], (nh, n_future)),
    ], axis=1)
    f = jnp.roll(f, -(QUERY_GROUP - 1), axis=1)
    flat = jnp.tile(f, (1, QUERY_GROUP))[:, :QUERY_GROUP * (period - 1)]
    toeplitz = flat.reshape(nh, QUERY_GROUP, period - 1)[:, :, :n_keys]
    qi = jnp.arange(QUERY_GROUP)[:, None]
    kj = jnp.arange(n_keys)[None, :]
    band_lo = (qi // CHUNK) * CHUNK
    in_band = (kj >= band_lo) & (kj < band_lo + BAND)
    return jnp.where(in_band[None], toeplitz * LOG2_E, NEG_INF).astype(F32)


def _mem_attn_kernel(xn_ref, wq_ref, mk_ref, mv_ref, wo_ref, x_ref, gp_ref, gn_ref,
                     xo_ref, xno_ref, acc0_ref, acc1_ref, *, n_tiles):
    def attend():
        q = jnp.dot(xn_ref[...], wq_ref[...], preferred_element_type=F32) * (MEM_HEAD_DIM ** -0.5 * LOG2_E)
        n_streams = mk_ref.shape[0]
        rows_per_stream = q.shape[0] // n_streams
        streams = []
        for b in range(n_streams):
            rows = slice(b * rows_per_stream, (b + 1) * rows_per_stream)
            heads = []
            for h in range(MEM_HEADS):
                c = slice(h * MEM_HEAD_DIM, (h + 1) * MEM_HEAD_DIM)
                head_rows = pl.ds(h, N_MEM, stride=MEM_HEADS)
                s = _qk(q[rows, c].astype(BF16), mk_ref[b, head_rows, :].astype(BF16))
                p = jnp.exp2(s - jnp.max(s, axis=-1, keepdims=True))
                l = jnp.sum(p, axis=-1, keepdims=True)
                o = jnp.dot(p.astype(BF16), mv_ref[b, head_rows, :].astype(BF16), preferred_element_type=F32) / l
                heads.append(o.astype(BF16))
            streams.append(jnp.concatenate(heads, axis=1))
        return streams[0] if n_streams == 1 else jnp.concatenate(streams, axis=0)

    _skewed_step(attend, wo_ref, x_ref, gp_ref, gn_ref, xo_ref, xno_ref, acc0_ref, acc1_ref, 1.0, n_tiles)


def _mem_attn(xn, x, mk, mv, wq, wo, layer, g4, batch, tm):
    t, d = x.shape
    n = t // tm
    seq = t // batch
    streams_per_tile = max(1, tm // seq)
    assert tm * n == t and (seq % tm == 0 or tm % seq == 0)
    cur = lambda i: (jnp.minimum(i, n - 1), 0)
    prev = lambda i: (jnp.maximum(i - 1, 0), 0)
    mem_spec = pl.BlockSpec(
        (None, streams_per_tile, N_MEM * MEM_HEADS, MEM_HEAD_DIM),
        lambda i: (layer, (jnp.minimum(i, n - 1) * tm) // (streams_per_tile * seq), 0, 0))
    return pl.pallas_call(
        functools.partial(_mem_attn_kernel, n_tiles=n),
        out_shape=[jax.ShapeDtypeStruct((t, d), F32), jax.ShapeDtypeStruct((t, d), BF16)],
        grid=(n + 1,),
        in_specs=[
            pl.BlockSpec((tm, d), cur),
            _layer_spec((d, MEM_WIDTH), layer),
            mem_spec, mem_spec,
            _layer_spec((MEM_WIDTH, d), layer),
            pl.BlockSpec((tm, d), prev),
            _gain_spec(layer, 5),
            _gain_spec(layer, 6),
        ],
        out_specs=[pl.BlockSpec((tm, d), prev), pl.BlockSpec((tm, d), prev)],
        scratch_shapes=[pltpu.VMEM((tm, d), F32), pltpu.VMEM((tm, d), F32)],
        compiler_params=_params(("arbitrary",)),
        name="mem_attn",
    )(xn, wq, mk, mv, wo, x, g4, g4)


def _mem_project_kernel(mem_ref, g_ref, wf_ref, k_ref, v_ref, w_ref):
    @pl.when(pl.program_id(1) == 0)
    def _():
        w_ref[...] = wf_ref[...].astype(BF16)

    kv = jnp.dot(_rms(mem_ref[...], g_ref[...]).astype(BF16), w_ref[...], preferred_element_type=F32)
    tm = kv.shape[0]
    for h in range(MEM_HEADS):
        head_rows = pl.ds(h, tm, stride=MEM_HEADS)
        k_ref[head_rows, :] = kv[:, h * MEM_HEAD_DIM:(h + 1) * MEM_HEAD_DIM]
        v_ref[head_rows, :] = kv[:, MEM_WIDTH + h * MEM_HEAD_DIM:MEM_WIDTH + (h + 1) * MEM_HEAD_DIM]


def _mem_project(mem, g_mem, w_kv, tm):
    t, d = mem.shape
    depth = w_kv.shape[0]
    out = jax.ShapeDtypeStruct((depth, t * MEM_HEADS, MEM_HEAD_DIM), F32)
    out_spec = pl.BlockSpec((None, tm * MEM_HEADS, MEM_HEAD_DIM), lambda l, i: (l, i, 0))
    return pl.pallas_call(
        _mem_project_kernel,
        out_shape=[out, out],
        grid=(depth, t // tm),
        in_specs=[
            pl.BlockSpec((tm, d), lambda l, i: (i, 0)),
            pl.BlockSpec((None, 1, d), lambda l, i: (l, 0, 0)),
            pl.BlockSpec((None, d, 2 * MEM_WIDTH), lambda l, i: (l, 0, 0)),
        ],
        out_specs=[out_spec, out_spec],
        scratch_shapes=[pltpu.VMEM((d, 2 * MEM_WIDTH), BF16)],
        compiler_params=_params(("arbitrary", "arbitrary")),
        name="mem_project",
    )(mem, g_mem, w_kv)


def _trunk(x, mem_k, mem_v, weights, *, batch, seq, conv_hist, band_cache, bias_tables, ffn_out_bf16=None):
    prompt = conv_hist is None
    (g4, w_ffn1_in, w_ffn1_out, w_ffn2_in, w_ffn2_out, w_conv_in, w_conv_dw, w_conv_out,
     w_attn_qkv, w_attn_o, w_mem_q, w_mem_o) = weights
    depth = g4.shape[0]
    tm_mem = TM_MEM if prompt else TM_MEM_SAMPLE
    cast_here = ffn_out_bf16 is None
    if cast_here:
        ffn_out_bf16 = []

    def ffn_in(xn, w_in, w_out, layer, which):
        if cast_here:
            a, w_out_bf16 = _ffn_in(xn, w_in, layer, w_out)
            ffn_out_bf16.append(w_out_bf16)
            return a, w_out_bf16
        return _ffn_in(xn, w_in, layer), ffn_out_bf16[2 * layer + which]

    xn = _rms_cast(x, g4)
    conv_state = band_k = band_v = None
    for i in range(depth):
        m = i // 2
        a, w_out = ffn_in(xn, w_ffn1_in, w_ffn1_out, i, 0)
        x, xn = _mm_res_norm(a, w_out, 0, x, g4, (i, 1), (i, 2), 0.5, TM_FFN_OUT)
        if i % 2 == 0:
            if prompt:
                z, conv_state = _conv_in_prompt(xn, w_conv_in, w_conv_dw, m, batch, seq)
            else:
                z, conv_state = _conv_in_sample(xn, w_conv_in, w_conv_dw, m, conv_hist, seq)
            x, xn = _mm_res_norm(z, w_conv_out, m, x, g4, (i, 3), (i, 4), 1.0, TM_MIX_OUT)
        else:
            if prompt:
                qkv_hm, band_k, band_v = _qkv(xn, w_attn_qkv, m, seq // TM_QKV)
                o = _band_attn_prompt(qkv_hm, bias_tables[0], batch, seq)
            else:
                qkv_hm, band_k, band_v = _qkv(xn, w_attn_qkv, m, 1)
                o = _band_attn_sample(qkv_hm, band_cache[0], band_cache[1], bias_tables[1], bias_tables[2],
                                      batch, seq)
            x, xn = _mm_res_norm(o, w_attn_o, m, x, g4, (i, 3), (i, 4), 1.0, TM_MIX_OUT)
        x, xn = _mem_attn(xn, x, mem_k, mem_v, w_mem_q, w_mem_o, i, g4, batch, tm_mem)
        a, w_out = ffn_in(xn, w_ffn2_in, w_ffn2_out, i, 1)
        last = i == depth - 1
        x, xn = _mm_res_norm(a, w_out, 0, x, g4, (i, 7), ((i + 1) % depth, 0), 0.5, TM_FFN_OUT,
                             emit_xn=not last)
    return x, conv_state, band_k, band_v, ffn_out_bf16


def kernel(x_prompt, x_sample, state_conv, cache_band_k, cache_band_v, cache_mem_k, cache_mem_v, mem_prompt, g_norm, g_mem, w_ffn1_in, w_ffn1_out, w_ffn2_in, w_ffn2_out, w_conv_in, w_conv_dw, w_conv_out, w_attn_qkv, rel_bias, w_attn_o, w_mem_q, w_mem_kv, w_mem_o):
    batch, seq, d = x_prompt.shape
    dec_batch, dec_seq, _ = x_sample.shape
    depth = g_norm.shape[0]
    assert depth == 2 and state_conv.shape[0] == 1 and cache_band_k.shape[0] == 1
    assert cache_band_k.shape[2] == PAST_WIN and dec_seq == CHUNK and seq % QUERY_GROUP == 0
    assert seq % TM_IN == 0 and seq >= PAST_WIN and TM_IN % dec_seq == 0 and TM_QKV % dec_seq == 0

    weights = (
        g_norm[:, :, None, :],
        w_ffn1_in, w_ffn1_out, w_ffn2_in, w_ffn2_out,
        w_conv_in, w_conv_dw, w_conv_out.astype(BF16),
        w_attn_qkv.astype(BF16), w_attn_o.astype(BF16), w_mem_q.astype(BF16), w_mem_o.astype(BF16),
    )
    table = _band_bias_table(rel_bias[0])
    bias_tables = (table, table[:, :CHUNK, :PAST_WIN], table[:, :CHUNK, PAST_WIN:BAND])

    mem_rows = N_MEM * MEM_HEADS
    mem_k_p, mem_v_p = _mem_project(mem_prompt.reshape(batch * N_MEM, d), g_mem[:, None, :], w_mem_kv, 512)

    y_p, conv_p, bk_p, bv_p, ffn_out_bf16 = _trunk(
        x_prompt.reshape(batch * seq, d),
        mem_k_p.reshape(depth, batch, mem_rows, MEM_HEAD_DIM),
        mem_v_p.reshape(depth, batch, mem_rows, MEM_HEAD_DIM), weights,
        batch=batch, seq=seq, conv_hist=None, band_cache=None, bias_tables=bias_tables)

    y_s, conv_s, bk_s, bv_s, _ = _trunk(
        x_sample.reshape(dec_batch * dec_seq, d),
        cache_mem_k.reshape(depth, dec_batch, mem_rows, MEM_HEAD_DIM),
        cache_mem_v.reshape(depth, dec_batch, mem_rows, MEM_HEAD_DIM), weights,
        batch=dec_batch, seq=dec_seq, conv_hist=state_conv[0],
        band_cache=(cache_band_k[0], cache_band_v[0]),
        bias_tables=bias_tables, ffn_out_bf16=ffn_out_bf16)

    mem_shape = (depth, batch, N_MEM, MEM_HEADS, MEM_HEAD_DIM)
    band_p_shape = (1, batch, PAST_WIN, N_HEADS, HEAD_DIM)
    band_s_shape = (1, dec_batch, dec_seq, N_HEADS, HEAD_DIM)
    return (y_p.reshape(batch, seq, d), y_s.reshape(dec_batch, dec_seq, d),
            conv_p[None], bk_p.reshape(band_p_shape), bv_p.reshape(band_p_shape),
            mem_k_p.reshape(mem_shape), mem_v_p.reshape(mem_shape),
            conv_s[None], bk_s.reshape(band_s_shape), bv_s.reshape(band_s_shape))
```

```python
import functools

import jax
import jax.numpy as jnp
from jax import lax
from jax.experimental import pallas as pl
from jax.experimental.pallas import tpu as pltpu

F32 = jnp.float32
BF16 = jnp.bfloat16

D_MODEL = 2048
D_FF = 5632
N_HEADS = 16
HEAD_DIM = 128
CHUNK = 64
PAST_WIN = 512
BAND = PAST_WIN + CHUNK
MAX_REL = 128
N_MEM = 256
MEM_HEADS = 4
MEM_HEAD_DIM = 128
MEM_WIDTH = MEM_HEADS * MEM_HEAD_DIM
EPS = 1e-6
NEG_INF = -1e30
LOG2_E = 1.4426950408889634

V7X_VMEM_LIMIT_BYTES = 58 * 1024 * 1024
BF16_SUBLANES = 16
LANES = 128
EPILOGUE_PIECES = 4
FF_TILE = 512
CONV_TILE = 512
TM_QKV = PAST_WIN
QUERY_GROUP = 4 * CHUNK
HEADS_PER_STEP = 4
TM_IN = 1024
TM_FFN_IN = 2048
TM_FFN_OUT = 256
TM_MIX_OUT = 512
TM_MEM = 512
TM_MEM_SAMPLE = 256


def _params(sem):
    return pltpu.CompilerParams(dimension_semantics=sem, vmem_limit_bytes=V7X_VMEM_LIMIT_BYTES)


def _layer_spec(shape, layer):
    zeros = (0,) * len(shape)
    return pl.BlockSpec((None,) + tuple(shape), lambda *_: (layer,) + zeros, pipeline_mode=pl.Buffered(1))


def _gain_spec(layer, n):
    return pl.BlockSpec((None, None, 1, D_MODEL), lambda *_: (layer, n, 0, 0), pipeline_mode=pl.Buffered(1))


def _rms(x, g):
    return x * lax.rsqrt(jnp.mean(x * x, axis=-1, keepdims=True) + EPS) * g


def _rms_cast_kernel(x_ref, g_ref, o_ref):
    o_ref[...] = _rms(x_ref[...], g_ref[...]).astype(BF16)


def _rms_cast(x, g4):
    t, d = x.shape
    return pl.pallas_call(
        _rms_cast_kernel,
        out_shape=jax.ShapeDtypeStruct((t, d), BF16),
        grid=(t // TM_IN,),
        in_specs=[pl.BlockSpec((TM_IN, d), lambda i: (i, 0)), _gain_spec(0, 0)],
        out_specs=pl.BlockSpec((TM_IN, d), lambda i: (i, 0)),
        compiler_params=_params(("arbitrary",)),
        name="rms_cast",
    )(x, g4)


def _res_norm_epilogue(acc_ref, x_ref, gp, gn, xo_ref, xno_ref, row0, n_rows):
    rows = BF16_SUBLANES
    xnew = None
    for r in range(row0, row0 + n_rows, rows):
        sl = slice(r, r + rows)
        xnew = x_ref[sl, :] + _rms(acc_ref[sl, :], gp)
        xo_ref[sl, :] = xnew
        if xno_ref is not None:
            xno_ref[sl, :] = _rms(xnew, gn).astype(BF16)
    return xnew[:, :LANES]


def _zero_from(x):
    sixteen = jnp.uint32(16)
    bits = lax.bitcast_convert_type(x, jnp.uint32)
    return lax.shift_right_logical(lax.shift_right_logical(bits, sixteen), sixteen).astype(F32)


def _after(lhs, zero):
    r = BF16_SUBLANES
    first = (lhs[:r, :LANES].astype(F32) + zero).astype(lhs.dtype)
    top = jnp.concatenate([first, lhs[:r, LANES:]], axis=1)
    return jnp.concatenate([top, lhs[r:, :]], axis=0)


def _skewed_step(load_lhs, w_ref, x_ref, gp_ref, gn_ref, xo_ref, xno_ref, acc0_ref, acc1_ref, scale, n_tiles):
    i = pl.program_id(0)
    tm, d = acc0_ref.shape
    rows = tm // EPILOGUE_PIECES
    cols = d // EPILOGUE_PIECES
    accs = (acc0_ref, acc1_ref)

    @pl.when(i == 0)
    def _():
        acc1_ref[...] = jnp.zeros_like(acc1_ref)

    def gains():
        return gp_ref[...] * scale, gn_ref[...]

    def step(prev_ref, cur_ref):
        gp, gn = gains()
        lhs = load_lhs()
        for k in range(EPILOGUE_PIECES):
            c = slice(k * cols, (k + 1) * cols)
            cur_ref[:, c] = jnp.dot(lhs, w_ref[:, c], preferred_element_type=F32)
            done = _res_norm_epilogue(prev_ref, x_ref, gp, gn, xo_ref, xno_ref, k * rows, rows)
            lhs = _after(lhs, _zero_from(done))

    for parity in (0, 1):
        @pl.when((i % 2 == parity) & (i < n_tiles))
        def _():
            step(accs[1 - parity], accs[parity])

    @pl.when(i == n_tiles)
    def _():
        gp, gn = gains()
        _res_norm_epilogue(accs[(n_tiles - 1) % 2], x_ref, gp, gn, xo_ref, xno_ref, 0, tm)


def _load_rows(a_ref):
    if len(a_ref.shape) == 2:
        return a_ref[...]
    return jnp.concatenate([a_ref[h] for h in range(a_ref.shape[0])], axis=1)


def _mm_res_norm_kernel(a_ref, w_ref, x_ref, gp_ref, gn_ref, xo_ref, *rest, scale, emit_xn, n_tiles):
    xno_ref, acc0_ref, acc1_ref = rest if emit_xn else (None,) + rest
    _skewed_step(lambda: _load_rows(a_ref), w_ref, x_ref, gp_ref, gn_ref, xo_ref, xno_ref,
                 acc0_ref, acc1_ref, scale, n_tiles)


def _mm_res_norm(a, w, w_layer, x, g4, post_gain, next_gain, scale, tm, emit_xn=True):
    t, d = x.shape
    k = w.shape[1]
    n = t // tm
    cur = lambda i: jnp.minimum(i, n - 1)
    prev = lambda i: (jnp.maximum(i - 1, 0), 0)
    if a.ndim == 2:
        a_spec = pl.BlockSpec((tm, k), lambda i: (cur(i), 0))
    else:
        a_spec = pl.BlockSpec((a.shape[0], tm, a.shape[2]), lambda i: (0, cur(i), 0))
    out_shape = [jax.ShapeDtypeStruct((t, d), F32)]
    out_specs = [pl.BlockSpec((tm, d), prev)]
    if emit_xn:
        out_shape.append(jax.ShapeDtypeStruct((t, d), BF16))
        out_specs.append(pl.BlockSpec((tm, d), prev))
    outs = pl.pallas_call(
        functools.partial(_mm_res_norm_kernel, scale=scale, emit_xn=emit_xn, n_tiles=n),
        out_shape=out_shape,
        grid=(n + 1,),
        in_specs=[
            a_spec,
            _layer_spec((k, d), w_layer),
            pl.BlockSpec((tm, d), prev),
            _gain_spec(*post_gain),
            _gain_spec(*next_gain),
        ],
        out_specs=out_specs,
        scratch_shapes=[pltpu.VMEM((tm, d), F32), pltpu.VMEM((tm, d), F32)],
        compiler_params=_params(("arbitrary",)),
        name="mm_res_norm",
    )(a, w, x, g4, g4)
    return (outs[0], outs[1]) if emit_xn else (outs[0], None)


def _ffn_in_kernel(xn_ref, wg_ref, wu_ref, *rest, cast_w_out):
    if cast_w_out:
        wo_f32_ref, o_ref, wo_bf16_ref, w_ref = rest
        wo_bf16_ref[...] = wo_f32_ref[...].astype(BF16)
    else:
        o_ref, w_ref = rest

    hc = FF_TILE // 2

    @pl.when(pl.program_id(1) == 0)
    def _():
        for c in range(2):
            w_ref[:, 2 * c * hc:(2 * c + 1) * hc] = wg_ref[:, c * hc:(c + 1) * hc].astype(BF16)
            w_ref[:, (2 * c + 1) * hc:(2 * c + 2) * hc] = wu_ref[:, c * hc:(c + 1) * hc].astype(BF16)

    half = xn_ref.shape[0] // 2
    for r in (slice(0, half), slice(half, 2 * half)):
        for c in range(2):
            h = jnp.dot(xn_ref[r, :], w_ref[:, 2 * c * hc:(2 * c + 2) * hc], preferred_element_type=F32)
            o_ref[r, c * hc:(c + 1) * hc] = (jax.nn.silu(h[:, :hc]) * h[:, hc:]).astype(BF16)


def _ffn_in(xn, w_in, layer, w_out=None):
    t, d = xn.shape
    nj = D_FF // FF_TILE
    ni = t // TM_FFN_IN
    in_specs = [
        pl.BlockSpec((TM_FFN_IN, d), lambda j, i: (i, 0)),
        pl.BlockSpec((None, d, FF_TILE), lambda j, i: (layer, 0, j)),
        pl.BlockSpec((None, d, FF_TILE), lambda j, i: (layer, 0, nj + j)),
    ]
    out_shape = [jax.ShapeDtypeStruct((t, D_FF), BF16)]
    out_specs = [pl.BlockSpec((TM_FFN_IN, FF_TILE), lambda j, i: (i, j))]
    operands = [xn, w_in, w_in]
    if w_out is not None:
        rows = D_FF // (nj * ni)
        assert rows * nj * ni == D_FF and rows % BF16_SUBLANES == 0
        in_specs.append(pl.BlockSpec((None, rows, d), lambda j, i: (layer, j * ni + i, 0)))
        out_shape.append(jax.ShapeDtypeStruct((1, D_FF, d), BF16))
        out_specs.append(pl.BlockSpec((None, rows, d), lambda j, i: (0, j * ni + i, 0)))
        operands.append(w_out)
    outs = pl.pallas_call(
        functools.partial(_ffn_in_kernel, cast_w_out=w_out is not None),
        out_shape=out_shape,
        grid=(nj, ni),
        in_specs=in_specs,
        out_specs=out_specs,
        scratch_shapes=[pltpu.VMEM((d, 2 * FF_TILE), BF16)],
        compiler_params=_params(("arbitrary", "arbitrary")),
        name="ffn_in",
    )(*operands)
    return outs if w_out is not None else outs[0]


def _qkv_kernel(xn_ref, w_ref, hm_ref, k_ref, v_ref, *, keep_every):
    j, i = pl.program_id(0), pl.program_id(1)
    acc = jnp.dot(xn_ref[...], w_ref[...], preferred_element_type=F32)
    heads = [acc[:, h * HEAD_DIM:(h + 1) * HEAD_DIM] for h in range(N_HEADS)]
    hm_scale = jnp.where(j == 0, HEAD_DIM ** -0.5 * LOG2_E, 1.0)
    for h in range(N_HEADS):
        hm_ref[h] = (heads[h] * hm_scale).astype(BF16)

    def keep(ref):
        ref[...] = pltpu.einshape("htd->thd", jnp.stack(heads))

    kept_tile = i % keep_every == keep_every - 1

    @pl.when((j == 1) & kept_tile)
    def _():
        keep(k_ref)

    @pl.when((j == 2) & kept_tile)
    def _():
        keep(v_ref)


def _qkv(xn, w_qkv, layer, keep_every):
    t, d = xn.shape
    n_keep = t // (TM_QKV * keep_every)
    kept = jax.ShapeDtypeStruct((n_keep, TM_QKV, N_HEADS, HEAD_DIM), F32)
    kept_block = (None, TM_QKV, N_HEADS, HEAD_DIM)

    def k_index(j, i):
        return (jnp.where(j < 1, 0, jnp.where(j > 1, n_keep - 1, i // keep_every)), 0, 0, 0)

    def v_index(j, i):
        return (jnp.where(j < 2, 0, i // keep_every), 0, 0, 0)

    return pl.pallas_call(
        functools.partial(_qkv_kernel, keep_every=keep_every),
        out_shape=[jax.ShapeDtypeStruct((3 * N_HEADS, t, HEAD_DIM), BF16), kept, kept],
        grid=(3, t // TM_QKV),
        in_specs=[
            pl.BlockSpec((TM_QKV, d), lambda j, i: (i, 0)),
            pl.BlockSpec((None, d, D_MODEL), lambda j, i: (layer, 0, j), pipeline_mode=pl.Buffered(1)),
        ],
        out_specs=[
            pl.BlockSpec((N_HEADS, TM_QKV, HEAD_DIM), lambda j, i: (j, i, 0)),
            pl.BlockSpec(kept_block, k_index),
            pl.BlockSpec(kept_block, v_index),
        ],
        compiler_params=_params(("arbitrary", "arbitrary")),
        name="qkv",
    )(xn, w_qkv)


def _cast_conv_weights(wb_ref, wc_ref, wx_ref, w_ref):
    @pl.when(pl.program_id(1) == 0)
    def _():
        for c in range(CONV_TILE // LANES):
            lane = slice(c * LANES, (c + 1) * LANES)
            for part, ref in enumerate((wb_ref, wc_ref, wx_ref)):
                w_ref[:, (3 * c + part) * LANES:(3 * c + part + 1) * LANES] = ref[:, lane].astype(BF16)


def _gated_conv(h, hist0, hist1, wdw, seq):
    tm = h.shape[0]
    nseq = tm // seq
    row = lax.broadcasted_iota(jnp.int32, (nseq, seq, LANES), 1)
    zs, us = [], []
    for c in range(CONV_TILE // LANES):
        lane = slice(c * LANES, (c + 1) * LANES)
        bg, cg, xv = (h[:, (3 * c + part) * LANES:(3 * c + part + 1) * LANES] for part in range(3))
        u = cg * xv
        u3 = u.reshape(nseq, seq, LANES)
        prev1 = pltpu.roll(u, 1, axis=0).reshape(u3.shape)
        prev2 = pltpu.roll(u, 2, axis=0).reshape(u3.shape)
        prev1 = jnp.where(row == 0, hist1[:, :, lane], prev1)
        prev2 = jnp.where(row == 0, hist0[:, :, lane], jnp.where(row == 1, hist1[:, :, lane], prev2))
        y = wdw[0:1, lane] * prev2 + wdw[1:2, lane] * prev1 + wdw[2:3, lane] * u3
        zs.append((bg.reshape(u3.shape) * y).astype(BF16).reshape(tm, LANES))
        us.append(u3)
    return jnp.concatenate(zs, axis=1), jnp.concatenate(us, axis=2)


def _conv_in_prompt_kernel(xn_ref, wb_ref, wc_ref, wx_ref, wdw_ref, z_ref, st_ref, w_ref, carry_ref,
                           *, tiles_per_seq):
    tm = xn_ref.shape[0]
    _cast_conv_weights(wb_ref, wc_ref, wx_ref, w_ref)

    @pl.when(pl.program_id(1) % tiles_per_seq == 0)
    def _():
        carry_ref[...] = jnp.zeros_like(carry_ref)

    h = jnp.dot(xn_ref[...], w_ref[...], preferred_element_type=F32)
    hist0 = carry_ref[0:1, :].reshape(1, 1, CONV_TILE)
    hist1 = carry_ref[1:2, :].reshape(1, 1, CONV_TILE)
    z, u3 = _gated_conv(h, hist0, hist1, wdw_ref[...], tm)
    z_ref[...] = z
    last = u3[0, tm - 2:, :]
    carry_ref[0:2, :] = last
    st_ref[0] = last


def _conv_weight_specs(d, layer):
    nj = d // CONV_TILE
    part = lambda p: pl.BlockSpec((None, d, CONV_TILE), lambda j, i: (layer, 0, p * nj + j),
                                  pipeline_mode=pl.Buffered(1))
    return [part(0), part(1), part(2), pl.BlockSpec((None, 3, CONV_TILE), lambda j, i: (layer, 0, j))]


def _conv_in_prompt(xn, w_in, w_dw, layer, batch, seq):
    t, d = xn.shape
    tiles_per_seq = seq // TM_IN
    return pl.pallas_call(
        functools.partial(_conv_in_prompt_kernel, tiles_per_seq=tiles_per_seq),
        out_shape=[jax.ShapeDtypeStruct((t, d), BF16), jax.ShapeDtypeStruct((batch, 2, d), F32)],
        grid=(d // CONV_TILE, t // TM_IN),
        in_specs=[pl.BlockSpec((TM_IN, d), lambda j, i: (i, 0))] + _conv_weight_specs(d, layer),
        out_specs=[
            pl.BlockSpec((TM_IN, CONV_TILE), lambda j, i: (i, j)),
            pl.BlockSpec((1, 2, CONV_TILE), lambda j, i: (i // tiles_per_seq, 0, j)),
        ],
        scratch_shapes=[pltpu.VMEM((d, 3 * CONV_TILE), BF16), pltpu.VMEM((8, CONV_TILE), F32)],
        compiler_params=_params(("arbitrary", "arbitrary")),
        name="conv_in_prompt",
    )(xn, w_in, w_in, w_in, w_dw)


def _conv_in_sample_kernel(xn_ref, wb_ref, wc_ref, wx_ref, wdw_ref, hist_ref, z_ref, st_ref, w_ref, *, seq):
    _cast_conv_weights(wb_ref, wc_ref, wx_ref, w_ref)
    h = jnp.dot(xn_ref[...], w_ref[...], preferred_element_type=F32)
    hist = hist_ref[...]
    z, u3 = _gated_conv(h, hist[:, 0:1, :], hist[:, 1:2, :], wdw_ref[...], seq)
    z_ref[...] = z
    st_ref[...] = u3[:, seq - 2:, :]


def _conv_in_sample(xn, w_in, w_dw, layer, hist, seq):
    t, d = xn.shape
    nseq = TM_IN // seq
    hist_spec = pl.BlockSpec((nseq, 2, CONV_TILE), lambda j, i: (i, 0, j))
    return pl.pallas_call(
        functools.partial(_conv_in_sample_kernel, seq=seq),
        out_shape=[jax.ShapeDtypeStruct((t, d), BF16), jax.ShapeDtypeStruct(hist.shape, F32)],
        grid=(d // CONV_TILE, t // TM_IN),
        in_specs=[pl.BlockSpec((TM_IN, d), lambda j, i: (i, 0))] + _conv_weight_specs(d, layer) + [hist_spec],
        out_specs=[pl.BlockSpec((TM_IN, CONV_TILE), lambda j, i: (i, j)), hist_spec],
        scratch_shapes=[pltpu.VMEM((d, 3 * CONV_TILE), BF16)],
        compiler_params=_params(("arbitrary", "arbitrary")),
        name="conv_in_sample",
    )(xn, w_in, w_in, w_in, w_dw, hist)


def _qk(q, k):
    return lax.dot_general(q, k, (((1,), (1,)), ((), ())), preferred_element_type=F32)


def _band_attn_prompt_kernel(q_ref, k_ref, v_ref, bias_ref, o_ref):
    seq = q_ref.shape[1]
    items = [(g, m) for g in range(q_ref.shape[0]) for m in range(seq // QUERY_GROUP)]

    def key_range(m):
        q0 = m * QUERY_GROUP
        return max(0, q0 - PAST_WIN), q0 + QUERY_GROUP

    def scores(g, m):
        q0 = m * QUERY_GROUP
        lo, hi = key_range(m)
        col0 = lo - (q0 - PAST_WIN)
        return _qk(q_ref[g, q0:hi, :], k_ref[g, lo:hi, :]) + bias_ref[g, :, col0:col0 + hi - lo]

    s_next = scores(*items[0])
    for n, (g, m) in enumerate(items):
        s = s_next
        if n + 1 < len(items):
            s_next = scores(*items[n + 1])
        lo, hi = key_range(m)
        q0 = m * QUERY_GROUP
        col0 = lo - (q0 - PAST_WIN)
        half = QUERY_GROUP // 2
        for r0 in (0, half):
            c_lo = max(r0 - col0, 0)
            c_hi = min(r0 + PAST_WIN + half, PAST_WIN + QUERY_GROUP) - col0
            sp = s[r0:r0 + half, c_lo:c_hi]
            p = jnp.exp2(sp - jnp.max(sp, axis=-1, keepdims=True))
            l = jnp.sum(p, axis=-1, keepdims=True)
            o = jnp.dot(p.astype(BF16), v_ref[g, lo + c_lo:lo + c_hi, :], preferred_element_type=F32) / l
            o_ref[g, q0 + r0:q0 + r0 + half, :] = o.astype(BF16)


def _band_attn_prompt(qkv_hm, bias, batch, seq):
    hps = HEADS_PER_STEP
    per_part = N_HEADS // hps
    head = lambda part: pl.BlockSpec((hps, seq, HEAD_DIM), lambda h, b: (part * per_part + h, b, 0))
    return pl.pallas_call(
        _band_attn_prompt_kernel,
        out_shape=jax.ShapeDtypeStruct((N_HEADS, batch * seq, HEAD_DIM), BF16),
        grid=(per_part, batch),
        in_specs=[head(0), head(1), head(2),
                  pl.BlockSpec((hps,) + bias.shape[1:], lambda h, b: (h, 0, 0))],
        out_specs=head(0),
        compiler_params=_params(("arbitrary", "arbitrary")),
        name="band_attn_prompt",
    )(qkv_hm, qkv_hm, qkv_hm, bias)


def _band_attn_sample_kernel(qkv_ref, ck_ref, cv_ref, bias_c_ref, bias_n_ref, o_ref):
    k_heads = pltpu.einshape("thd->htd", ck_ref[...]).astype(BF16)
    v_heads = pltpu.einshape("thd->htd", cv_ref[...]).astype(BF16)
    for h in range(N_HEADS):
        q, kn, vn = qkv_ref[h], qkv_ref[N_HEADS + h], qkv_ref[2 * N_HEADS + h]
        kc = k_heads[h]
        vc = v_heads[h]
        s_c = _qk(q, kc) + bias_c_ref[h]
        s_n = _qk(q, kn) + bias_n_ref[h]
        mx = jnp.maximum(jnp.max(s_c, axis=-1, keepdims=True), jnp.max(s_n, axis=-1, keepdims=True))
        p_c = jnp.exp2(s_c - mx)
        p_n = jnp.exp2(s_n - mx)
        l = jnp.sum(p_c, axis=-1, keepdims=True) + jnp.sum(p_n, axis=-1, keepdims=True)
        o = (jnp.dot(p_c.astype(BF16), vc, preferred_element_type=F32)
             + jnp.dot(p_n.astype(BF16), vn, preferred_element_type=F32)) / l
        o_ref[h] = o.astype(BF16)


def _band_attn_sample(qkv_hm, ck, cv, bias_c, bias_n, batch, seq):
    cache = pl.BlockSpec((None, PAST_WIN, N_HEADS, HEAD_DIM), lambda b: (b, 0, 0, 0))
    const = lambda a: pl.BlockSpec(a.shape, lambda b: (0,) * a.ndim, pipeline_mode=pl.Buffered(1))
    return pl.pallas_call(
        _band_attn_sample_kernel,
        out_shape=jax.ShapeDtypeStruct((N_HEADS, batch * seq, HEAD_DIM), BF16),
        grid=(batch,),
        in_specs=[
            pl.BlockSpec((3 * N_HEADS, seq, HEAD_DIM), lambda b: (0, b, 0)),
            cache, cache,
            const(bias_c), const(bias_n),
        ],
        out_specs=pl.BlockSpec((N_HEADS, seq, HEAD_DIM), lambda b: (0, b, 0)),
        compiler_params=_params(("arbitrary",)),
        name="band_attn_sample",
    )(qkv_hm, ck, cv, bias_c, bias_n)


def _band_bias_table(rel_bias):
    nh = rel_bias.shape[0]
    n_keys = PAST_WIN + QUERY_GROUP
    period = n_keys + QUERY_GROUP
    n_far_past = PAST_WIN - MAX_REL + QUERY_GROUP - 1
    n_future = period - n_far_past - rel_bias.shape[1]
    f = jnp.concatenate([
        jnp.broadcast_to(rel_bias[:, -1:], (nh, n_far_past)),
        rel_bias[:, ::-1],
        jnp.broadcast_to(rel_bias[:, :1], (nh, n_future)),
    ], axis=1)
    f = jnp.roll(f, -(QUERY_GROUP - 1), axis=1)
    flat = jnp.tile(f, (1, QUERY_GROUP))[:, :QUERY_GROUP * (period - 1)]
    toeplitz = flat.reshape(nh, QUERY_GROUP, period - 1)[:, :, :n_keys]
    qi = jnp.arange(QUERY_GROUP)[:, None]
    kj = jnp.arange(n_keys)[None, :]
    band_lo = (qi // CHUNK) * CHUNK
    in_band = (kj >= band_lo) & (kj < band_lo + BAND)
    return jnp.where(in_band[None], toeplitz * LOG2_E, NEG_INF).astype(F32)


def _mem_attn_kernel(xn_ref, wq_ref, mk_ref, mv_ref, wo_ref, x_ref, gp_ref, gn_ref,
                     xo_ref, xno_ref, acc0_ref, acc1_ref, *, n_tiles):
    def attend():
        q = jnp.dot(xn_ref[...], wq_ref[...], preferred_element_type=F32) * (MEM_HEAD_DIM ** -0.5 * LOG2_E)
        n_streams = mk_ref.shape[0]
        rows_per_stream = q.shape[0] // n_streams
        streams = []
        for b in range(n_streams):
            rows = slice(b * rows_per_stream, (b + 1) * rows_per_stream)
            heads = []
            for h in range(MEM_HEADS):
                c = slice(h * MEM_HEAD_DIM, (h + 1) * MEM_HEAD_DIM)
                head_rows = pl.ds(h, N_MEM, stride=MEM_HEADS)
                s = _qk(q[rows, c].astype(BF16), mk_ref[b, head_rows, :].astype(BF16))
                p = jnp.exp2(s - jnp.max(s, axis=-1, keepdims=True))
                l = jnp.sum(p, axis=-1, keepdims=True)
                o = jnp.dot(p.astype(BF16), mv_ref[b, head_rows, :].astype(BF16), preferred_element_type=F32) / l
                heads.append(o.astype(BF16))
            streams.append(jnp.concatenate(heads, axis=1))
        return streams[0] if n_streams == 1 else jnp.concatenate(streams, axis=0)

    _skewed_step(attend, wo_ref, x_ref, gp_ref, gn_ref, xo_ref, xno_ref, acc0_ref, acc1_ref, 1.0, n_tiles)


def _mem_attn(xn, x, mk, mv, wq, wo, layer, g4, batch, tm):
    t, d = x.shape
    n = t // tm
    seq = t // batch
    streams_per_tile = max(1, tm // seq)
    assert tm * n == t and (seq % tm == 0 or tm % seq == 0)
    cur = lambda i: (jnp.minimum(i, n - 1), 0)
    prev = lambda i: (jnp.maximum(i - 1, 0), 0)
    mem_spec = pl.BlockSpec(
        (None, streams_per_tile, N_MEM * MEM_HEADS, MEM_HEAD_DIM),
        lambda i: (layer, (jnp.minimum(i, n - 1) * tm) // (streams_per_tile * seq), 0, 0))
    return pl.pallas_call(
        functools.partial(_mem_attn_kernel, n_tiles=n),
        out_shape=[jax.ShapeDtypeStruct((t, d), F32), jax.ShapeDtypeStruct((t, d), BF16)],
        grid=(n + 1,),
        in_specs=[
            pl.BlockSpec((tm, d), cur),
            _layer_spec((d, MEM_WIDTH), layer),
            mem_spec, mem_spec,
            _layer_spec((MEM_WIDTH, d), layer),
            pl.BlockSpec((tm, d), prev),
            _gain_spec(layer, 5),
            _gain_spec(layer, 6),
        ],
        out_specs=[pl.BlockSpec((tm, d), prev), pl.BlockSpec((tm, d), prev)],
        scratch_shapes=[pltpu.VMEM((tm, d), F32), pltpu.VMEM((tm, d), F32)],
        compiler_params=_params(("arbitrary",)),
        name="mem_attn",
    )(xn, wq, mk, mv, wo, x, g4, g4)


def _mem_project_kernel(mem_ref, g_ref, wf_ref, k_ref, v_ref, w_ref):
    @pl.when(pl.program_id(1) == 0)
    def _():
        w_ref[...] = wf_ref[...].astype(BF16)

    kv = jnp.dot(_rms(mem_ref[...], g_ref[...]).astype(BF16), w_ref[...], preferred_element_type=F32)
    tm = kv.shape[0]
    for h in range(MEM_HEADS):
        head_rows = pl.ds(h, tm, stride=MEM_HEADS)
        k_ref[head_rows, :] = kv[:, h * MEM_HEAD_DIM:(h + 1) * MEM_HEAD_DIM]
        v_ref[head_rows, :] = kv[:, MEM_WIDTH + h * MEM_HEAD_DIM:MEM_WIDTH + (h + 1) * MEM_HEAD_DIM]


def _mem_project(mem, g_mem, w_kv, tm):
    t, d = mem.shape
    depth = w_kv.shape[0]
    out = jax.ShapeDtypeStruct((depth, t * MEM_HEADS, MEM_HEAD_DIM), F32)
    out_spec = pl.BlockSpec((None, tm * MEM_HEADS, MEM_HEAD_DIM), lambda l, i: (l, i, 0))
    return pl.pallas_call(
        _mem_project_kernel,
        out_shape=[out, out],
        grid=(depth, t // tm),
        in_specs=[
            pl.BlockSpec((tm, d), lambda l, i: (i, 0)),
            pl.BlockSpec((None, 1, d), lambda l, i: (l, 0, 0)),
            pl.BlockSpec((None, d, 2 * MEM_WIDTH), lambda l, i: (l, 0, 0)),
        ],
        out_specs=[out_spec, out_spec],
        scratch_shapes=[pltpu.VMEM((d, 2 * MEM_WIDTH), BF16)],
        compiler_params=_params(("arbitrary", "arbitrary")),
        name="mem_project",
    )(mem, g_mem, w_kv)


def _trunk(x, mem_k, mem_v, weights, *, batch, seq, conv_hist, band_cache, bias_tables, ffn_out_bf16=None):
    prompt = conv_hist is None
    (g4, w_ffn1_in, w_ffn1_out, w_ffn2_in, w_ffn2_out, w_conv_in, w_conv_dw, w_conv_out,
     w_attn_qkv, w_attn_o, w_mem_q, w_mem_o) = weights
    depth = g4.shape[0]
    tm_mem = TM_MEM if prompt else TM_MEM_SAMPLE
    cast_here = ffn_out_bf16 is None
    if cast_here:
        ffn_out_bf16 = []

    def ffn_in(xn, w_in, w_out, layer, which):
        if cast_here:
            a, w_out_bf16 = _ffn_in(xn, w_in, layer, w_out)
            ffn_out_bf16.append(w_out_bf16)
            return a, w_out_bf16
        return _ffn_in(xn, w_in, layer), ffn_out_bf16[2 * layer + which]

    xn = _rms_cast(x, g4)
    conv_state = band_k = band_v = None
    for i in range(depth):
        m = i // 2
        a, w_out = ffn_in(xn, w_ffn1_in, w_ffn1_out, i, 0)
        x, xn = _mm_res_norm(a, w_out, 0, x, g4, (i, 1), (i, 2), 0.5, TM_FFN_OUT)
        if i % 2 == 0:
            if prompt:
                z, conv_state = _conv_in_prompt(xn, w_conv_in, w_conv_dw, m, batch, seq)
            else:
                z, conv_state = _conv_in_sample(xn, w_conv_in, w_conv_dw, m, conv_hist, seq)
            x, xn = _mm_res_norm(z, w_conv_out, m, x, g4, (i, 3), (i, 4), 1.0, TM_MIX_OUT)
        else:
            if prompt:
                qkv_hm, band_k, band_v = _qkv(xn, w_attn_qkv, m, seq // TM_QKV)
                o = _band_attn_prompt(qkv_hm, bias_tables[0], batch, seq)
            else:
                qkv_hm, band_k, band_v = _qkv(xn, w_attn_qkv, m, 1)
                o = _band_attn_sample(qkv_hm, band_cache[0], band_cache[1], bias_tables[1], bias_tables[2],
                                      batch, seq)
            x, xn = _mm_res_norm(o, w_attn_o, m, x, g4, (i, 3), (i, 4), 1.0, TM_MIX_OUT)
        x, xn = _mem_attn(xn, x, mem_k, mem_v, w_mem_q, w_mem_o, i, g4, batch, tm_mem)
        a, w_out = ffn_in(xn, w_ffn2_in, w_ffn2_out, i, 1)
        last = i == depth - 1
        x, xn = _mm_res_norm(a, w_out, 0, x, g4, (i, 7), ((i + 1) % depth, 0), 0.5, TM_FFN_OUT,
                             emit_xn=not last)
    return x, conv_state, band_k, band_v, ffn_out_bf16


def kernel(x_prompt, x_sample, state_conv, cache_band_k, cache_band_v, cache_mem_k, cache_mem_v, mem_prompt, g_norm, g_mem, w_ffn1_in, w_ffn1_out, w_ffn2_in, w_ffn2_out, w_conv_in, w_conv_dw, w_conv_out, w_attn_qkv, rel_bias, w_attn_o, w_mem_q, w_mem_kv, w_mem_o):
    batch, seq, d = x_prompt.shape
    dec_batch, dec_seq, _ = x_sample.shape
    depth = g_norm.shape[0]
    assert depth == 2 and state_conv.shape[0] == 1 and cache_band_k.shape[0] == 1
    assert cache_band_k.shape[2] == PAST_WIN and dec_seq == CHUNK and seq % QUERY_GROUP == 0
    assert seq % TM_IN == 0 and seq >= PAST_WIN and TM_IN % dec_seq == 0 and TM_QKV % dec_seq == 0

    weights = (
        g_norm[:, :, None, :],
        w_ffn1_in, w_ffn1_out, w_ffn2_in, w_ffn2_out,
        w_conv_in, w_conv_dw, w_conv_out.astype(BF16),
        w_attn_qkv.astype(BF16), w_attn_o.astype(BF16), w_mem_q.astype(BF16), w_mem_o.astype(BF16),
    )
    table = _band_bias_table(rel_bias[0])
    bias_tables = (table, table[:, :CHUNK, :PAST_WIN], table[:, :CHUNK, PAST_WIN:BAND])

    mem_rows = N_MEM * MEM_HEADS
    mem_k_p, mem_v_p = _mem_project(mem_prompt.reshape(batch * N_MEM, d), g_mem[:, None, :], w_mem_kv, 512)

    y_p, conv_p, bk_p, bv_p, ffn_out_bf16 = _trunk(
        x_prompt.reshape(batch * seq, d),
        mem_k_p.reshape(depth, batch, mem_rows, MEM_HEAD_DIM),
        mem_v_p.reshape(depth, batch, mem_rows, MEM_HEAD_DIM), weights,
        batch=batch, seq=seq, conv_hist=None, band_cache=None, bias_tables=bias_tables)

    y_s, conv_s, bk_s, bv_s, _ = _trunk(
        x_sample.reshape(dec_batch * dec_seq, d),
        cache_mem_k.reshape(depth, dec_batch, mem_rows, MEM_HEAD_DIM),
        cache_mem_v.reshape(depth, dec_batch, mem_rows, MEM_HEAD_DIM), weights,
        batch=dec_batch, seq=dec_seq, conv_hist=state_conv[0],
        band_cache=(cache_band_k[0], cache_band_v[0]),
        bias_tables=bias_tables, ffn_out_bf16=ffn_out_bf16)

    mem_shape = (depth, batch, N_MEM, MEM_HEADS, MEM_HEAD_DIM)
    band_p_shape = (1, batch, PAST_WIN, N_HEADS, HEAD_DIM)
    band_s_shape = (1, dec_batch, dec_seq, N_HEADS, HEAD_DIM)
    return (y_p.reshape(batch, seq, d), y_s.reshape(dec_batch, dec_seq, d),
            conv_p[None], bk_p.reshape(band_p_shape), bv_p.reshape(band_p_shape),
            mem_k_p.reshape(mem_shape), mem_v_p.reshape(mem_shape),
            conv_s[None], bk_s.reshape(band_s_shape), bv_s.reshape(band_s_shape))
```
